```python
import math
import jax, jax.numpy as jnp
from jax import lax
import numpy as np

D_MODEL = 1024
BATCH = 8
SEQ = 4096
DEPTH = 4

CHUNK = 64
EPS = 1e-6

S5_WIDTH = D_MODEL // 2
S5_GROUP = 16
S5_GROUPS = S5_WIDTH // S5_GROUP
S5_STATE = 64
DT_MIN = 1e-3
DT_MAX = 1e-1

GLA_HEADS = 4
GLA_KEY = D_MODEL // 2
GLA_VAL = D_MODEL
GLA_DK = GLA_KEY // GLA_HEADS
GLA_DV = GLA_VAL // GLA_HEADS
GLA_GATE_RANK = 16
GLA_GATE_TEMP = 16.0

D_FF = -(-8 * D_MODEL // (3 * 256)) * 256

IN_PROJ_SIZES = (S5_WIDTH, GLA_KEY, GLA_KEY, GLA_VAL, GLA_VAL, GLA_GATE_RANK, D_MODEL, D_MODEL)
D_IN_PROJ = S5_WIDTH + 2 * GLA_KEY + 2 * GLA_VAL + GLA_GATE_RANK + 2 * D_MODEL

kernel_name = "hybrid_s5_gla_gated_merge_trunk"


def rms_norm(x, g):
    x32 = x.astype(jnp.float32)
    y = x32 * lax.rsqrt(jnp.mean(x32 * x32, axis=-1, keepdims=True) + EPS)
    return (y * g.astype(jnp.float32)).astype(x.dtype)


def s5_branch(u, a_re, a_im, log_dt, b_re, b_im, c_re, c_im, d_skip, w_glu):
    f32 = jnp.float32
    bsz, seq, _ = u.shape
    u32 = u.astype(f32).reshape(bsz, seq, S5_GROUPS, S5_GROUP)
    a = lax.complex(a_re.astype(f32), a_im.astype(f32))
    dt = jnp.exp(log_dt.astype(f32))[:, None]
    a_bar = jnp.exp(a * dt)
    b = lax.complex(b_re.astype(f32), b_im.astype(f32))
    b_bar = ((a_bar - 1.0) / a)[..., None] * b
    bu = jnp.einsum('blgc,gpc->blgp', u32.astype(jnp.complex64), b_bar)
    decay = jnp.broadcast_to(a_bar, bu.shape)

    def combine(left, right):
        a_l, b_l = left
        a_r, b_r = right
        return a_r * a_l, a_r * b_l + b_r

    _, states = lax.associative_scan(combine, (decay, bu), axis=1)
    c = lax.complex(c_re.astype(f32), c_im.astype(f32))
    y = jnp.real(jnp.einsum('blgp,gcp->blgc', states, c)) \
        + d_skip.astype(f32).reshape(S5_GROUPS, S5_GROUP) * u32
    y = jax.nn.gelu(y.reshape(bsz, seq, S5_WIDTH))
    val, gate = jnp.split(y @ w_glu.astype(f32), 2, axis=-1)
    return (val * jax.nn.sigmoid(gate)).astype(u.dtype)


def gla_branch(q, k, v, g, a_low, w_gate_up, b_gate, head_norm_g):
    f32 = jnp.float32
    out_dtype = v.dtype
    bsz, seq, _ = q.shape
    n_chunks = seq // CHUNK

    def chunked(t, d):
        return t.astype(f32).reshape(bsz, n_chunks, CHUNK, GLA_HEADS, d)

    qc = chunked(q, GLA_DK) * (GLA_DK ** -0.5)
    kc = chunked(k, GLA_DK)
    vc = chunked(v, GLA_DV)
    log_alpha = jax.nn.log_sigmoid(a_low.astype(f32) @ w_gate_up.astype(f32)
                                   + b_gate.astype(f32)) / GLA_GATE_TEMP
    log_alpha = chunked(log_alpha, GLA_DK)
    cum = jnp.cumsum(log_alpha, axis=2)
    total = cum[:, :, -1:]
    k_end = kc * jnp.exp(total - cum)

    scores = jnp.einsum('bcqhk,bcshk->bchqs', qc, k_end)
    intra = jnp.einsum('bchqs,bcshv->bcqhv', scores, vc)

    kv = jnp.einsum('bcshk,bcshv->bchkv', k_end, vc)
    chunk_decay = jnp.exp(total[:, :, 0])

    def step(state, xs):
        dec, kv_c = xs
        return dec[..., None] * state + kv_c, state

    init = jnp.zeros((bsz, GLA_HEADS, GLA_DK, GLA_DV), f32)
    _, prev = lax.scan(step, init, (jnp.moveaxis(chunk_decay, 1, 0), jnp.moveaxis(kv, 1, 0)))
    prev = jnp.moveaxis(prev, 0, 1)
    inter = jnp.einsum('bcqhk,bchkv->bcqhv', qc * jnp.exp(total), prev)

    o = intra + inter
    o = o * lax.rsqrt(jnp.mean(o * o, axis=-1, keepdims=True) + EPS)
    o = o * head_norm_g.astype(f32).reshape(GLA_HEADS, GLA_DV)
    o = o.reshape(bsz, seq, GLA_VAL) * jax.nn.silu(g.astype(f32))
    return o.astype(out_dtype)


def setup_inputs(seed: int = 0) -> dict:
    key = jax.random.key(seed)
    ks = jax.random.split(key, 24)
    f32 = jnp.float32
    nrm = lambda k, shape, scale: scale * jax.random.normal(k, shape, f32)
    gain = lambda k, shape: 1.0 + 0.02 * jax.random.normal(k, shape, f32)
    a_im_init = jnp.pi * jnp.arange(S5_STATE, dtype=f32)
    return {
        "x": jax.random.normal(ks[0], (BATCH, SEQ, D_MODEL), f32),
        "attn_norm_g": gain(ks[1], (DEPTH, D_MODEL)),
        "w_in": nrm(ks[2], (DEPTH, D_MODEL, D_IN_PROJ), D_MODEL ** -0.5),
        "s5_a_re": -0.5 + 0.01 * jax.random.normal(ks[3], (DEPTH, S5_GROUPS, S5_STATE), f32),
        "s5_a_im": a_im_init + 0.01 * jax.random.normal(ks[4], (DEPTH, S5_GROUPS, S5_STATE), f32),
        "s5_log_dt": jax.random.uniform(ks[5], (DEPTH, S5_GROUPS), f32,
                                        minval=math.log(DT_MIN), maxval=math.log(DT_MAX)),
        "s5_b_re": nrm(ks[6], (DEPTH, S5_GROUPS, S5_STATE, S5_GROUP), (2 * S5_GROUP) ** -0.5),
        "s5_b_im": nrm(ks[7], (DEPTH, S5_GROUPS, S5_STATE, S5_GROUP), (2 * S5_GROUP) ** -0.5),
        "s5_c_re": nrm(ks[8], (DEPTH, S5_GROUPS, S5_GROUP, S5_STATE), S5_STATE ** -0.5),
        "s5_c_im": nrm(ks[9], (DEPTH, S5_GROUPS, S5_GROUP, S5_STATE), S5_STATE ** -0.5),
        "s5_d": nrm(ks[10], (DEPTH, S5_WIDTH), 1.0),
        "s5_w_glu": nrm(ks[11], (DEPTH, S5_WIDTH, 2 * S5_WIDTH), S5_WIDTH ** -0.5),
        "gla_w_gate_up": nrm(ks[12], (DEPTH, GLA_GATE_RANK, GLA_KEY), GLA_GATE_RANK ** -0.5),
        "gla_b_gate": nrm(ks[13], (DEPTH, GLA_KEY), 0.1),
        "gla_head_norm_g": gain(ks[14], (DEPTH, GLA_VAL)),
        "w_branch_s5": nrm(ks[15], (DEPTH, S5_WIDTH, D_MODEL), S5_WIDTH ** -0.5),
        "w_branch_gla": nrm(ks[16], (DEPTH, GLA_VAL, D_MODEL), GLA_VAL ** -0.5),
        "w_out": nrm(ks[17], (DEPTH, D_MODEL, D_MODEL), D_MODEL ** -0.5),
        "ffn_norm_g": gain(ks[18], (DEPTH, D_MODEL)),
        "w_ffn_gate": nrm(ks[19], (DEPTH, D_MODEL, D_FF), D_MODEL ** -0.5),
        "w_ffn_up": nrm(ks[20], (DEPTH, D_MODEL, D_FF), D_MODEL ** -0.5),
        "w_ffn_down": nrm(ks[21], (DEPTH, D_FF, D_MODEL), D_FF ** -0.5),
        "final_norm_g": gain(ks[22], (D_MODEL,)),
    }


def reference(x, attn_norm_g, w_in, s5_a_re, s5_a_im, s5_log_dt, s5_b_re, s5_b_im, s5_c_re, s5_c_im,
              s5_d, s5_w_glu, gla_w_gate_up, gla_b_gate, gla_head_norm_g, w_branch_s5, w_branch_gla,
              w_out, ffn_norm_g, w_ffn_gate, w_ffn_up, w_ffn_down, final_norm_g):
    split_points = [int(p) for p in np.cumsum(IN_PROJ_SIZES)[:-1]]
    h = x
    for layer in range(DEPTH):
        xn = rms_norm(h, attn_norm_g[layer])
        proj = xn @ w_in[layer]
        u, q, k, v, g, a_low, gate_s5, gate_gla = jnp.split(proj, split_points, axis=-1)
        y_s5 = s5_branch(u, s5_a_re[layer], s5_a_im[layer], s5_log_dt[layer], s5_b_re[layer],
                         s5_b_im[layer], s5_c_re[layer], s5_c_im[layer], s5_d[layer], s5_w_glu[layer])
        y_gla = gla_branch(q, k, v, g, a_low, gla_w_gate_up[layer], gla_b_gate[layer],
                           gla_head_norm_g[layer])
        mixed = jax.nn.sigmoid(gate_s5) * (y_s5 @ w_branch_s5[layer]) \
            + jax.nn.sigmoid(gate_gla) * (y_gla @ w_branch_gla[layer])
        h = h + mixed @ w_out[layer]
        hn = rms_norm(h, ffn_norm_g[layer])
        h = h + (jax.nn.silu(hn @ w_ffn_gate[layer]) * (hn @ w_ffn_up[layer])) @ w_ffn_down[layer]
    return rms_norm(h, final_norm_g)
```

```python
import functools
import math

import numpy as np
import jax
import jax.numpy as jnp
from jax import lax
from jax.experimental import pallas as pl
from jax.experimental.pallas import tpu as pltpu

F32 = jnp.float32
BF16 = jnp.bfloat16

D_MODEL = 1024
EPS = 1e-6
CHUNK = 64

S5_WIDTH = 512
S5_GROUP = 16
S5_GROUPS = 32
S5_STATE = 64
S5_HALF_GROUPS = S5_GROUPS // 2
S5_HALF_COLS = S5_HALF_GROUPS * S5_STATE

GLA_HEADS = 4
GLA_KEY = 512
GLA_VAL = 1024
GLA_DK = 128
GLA_DV = 256
GLA_GATE_RANK = 16
GLA_GATE_TEMP = 16.0

D_FF = 2816
FF_CHUNKS = ((0, 1024), (1024, 2048), (2048, 2816))

_OFF_U, _OFF_Q, _OFF_K, _OFF_V, _OFF_G, _OFF_GS5, _OFF_GGLA, _OFF_ALOW, _W_IN_COLS = (
    0, 512, 1024, 1536, 2560, 3584, 4608, 5632, 5760)
_ALOW_PAD = _W_IN_COLS - _OFF_ALOW

ROW_TILE = 512
S5_T = 64
S5_ROWS = 8 * S5_T
GLA_T = 512
GLA_NCH = GLA_T // CHUNK

VMEM_LIMIT = 56 * 1024 * 1024


def _dot(a, b):
    return jnp.dot(a, b, preferred_element_type=F32)


def _split_bf16(x):
    hi = x.astype(BF16)
    lo = (x - hi.astype(F32)).astype(BF16)
    return hi, lo


def _rms(x, g):
    ms = jnp.mean(x * x, axis=-1, keepdims=True)
    return x * lax.rsqrt(ms + EPS) * g


def _const_spec(shape):
    zeros = (0,) * len(shape)
    return pl.BlockSpec(shape, lambda *_: zeros)


def _inproj_kernel(x_ref, g_ref, w_ref, wup_ref, bg_ref,
                   u_ref, q_ref, k_ref, la_ref, v_ref, sg_ref, ss5_ref, sgla_ref):
    xn = _rms(x_ref[...], g_ref[...]).astype(BF16)

    def seg(a, b):
        return _dot(xn, w_ref[:, a:b])

    u_ref[...] = seg(_OFF_U, _OFF_Q)
    q_ref[...] = (seg(_OFF_Q, _OFF_K) * (GLA_DK ** -0.5)).astype(BF16)
    k_ref[...] = seg(_OFF_K, _OFF_V)
    v_ref[...] = seg(_OFF_V, _OFF_G).astype(BF16)
    g = seg(_OFF_G, _OFF_GS5)
    sg_ref[...] = (g * jax.nn.sigmoid(g)).astype(BF16)
    ss5_ref[...] = jax.nn.sigmoid(seg(_OFF_GS5, _OFF_GGLA)).astype(BF16)
    sgla_ref[...] = jax.nn.sigmoid(seg(_OFF_GGLA, _OFF_ALOW)).astype(BF16)
    a_low = seg(_OFF_ALOW, _W_IN_COLS)
    z = _dot(a_low.astype(BF16), wup_ref[...]) + bg_ref[...]
    log_sig = jnp.minimum(z, 0.0) - jnp.log1p(jnp.exp(-jnp.abs(z)))
    la_ref[...] = log_sig * (1.0 / GLA_GATE_TEMP)


def _inproj(h, g, w, wup, bg):
    n = h.shape[0]
    row = lambda i: (i, 0)
    outs = [
        (S5_WIDTH, F32), (GLA_KEY, BF16), (GLA_KEY, F32), (GLA_KEY, F32),
        (GLA_VAL, BF16), (GLA_VAL, BF16), (D_MODEL, BF16), (D_MODEL, BF16)]
    return pl.pallas_call(
        _inproj_kernel,
        grid=(n // ROW_TILE,),
        in_specs=[
            pl.BlockSpec((ROW_TILE, D_MODEL), row),
            _const_spec((1, D_MODEL)),
            _const_spec((D_MODEL, _W_IN_COLS)),
            _const_spec((_ALOW_PAD, GLA_KEY)),
            _const_spec((1, GLA_KEY)),
        ],
        out_specs=[pl.BlockSpec((ROW_TILE, c), row) for c, _ in outs],
        out_shape=[jax.ShapeDtypeStruct((n, c), dt) for c, dt in outs],
        compiler_params=pltpu.CompilerParams(
            dimension_semantics=("arbitrary",), vmem_limit_bytes=VMEM_LIMIT),
        name="inproj",
    )(h, g, w, wup, bg)


def _gelu_tanh(x):
    c = math.sqrt(2.0 / math.pi)
    return 0.5 * x * (1.0 + jnp.tanh(c * (x + 0.044715 * (x * x * x))))


def _s5_kernel(u_ref, perm_ref, permt_ref, bmat_ref, cmat_ref, ar_ref, ai_ref, d_ref, wglu_ref,
               o_ref, buf_ref, st_ref):
    @pl.when(pl.program_id(0) == 0)
    def _():
        st_ref[...] = jnp.zeros_like(st_ref)

    u_hi, u_lo = _split_bf16(u_ref[...].reshape(S5_ROWS, S5_WIDTH))
    perm = perm_ref[...]
    ut_hi = _dot(perm, u_hi)
    ut = ut_hi + _dot(perm, u_lo)
    ut_bf = ut_hi.astype(BF16)

    half_w = 2 * S5_HALF_COLS
    for kt in range(2):
        buf_ref[:, kt * half_w:(kt + 1) * half_w] = _dot(
            ut_bf[:, kt * 256:(kt + 1) * 256], bmat_ref[kt])

    for kt in range(2):
        c_re = kt * half_w
        c_im = c_re + S5_HALF_COLS
        ar = ar_ref[:, kt * S5_HALF_COLS:(kt + 1) * S5_HALF_COLS]
        ai = ai_ref[:, kt * S5_HALF_COLS:(kt + 1) * S5_HALF_COLS]

        def body(t, carry, c_re=c_re, c_im=c_im, ar=ar, ai=ai):
            s_re, s_im = carry
            r = pl.multiple_of(t * 8, 8)
            n_re = ar * s_re - ai * s_im + buf_ref[pl.ds(r, 8), c_re:c_re + S5_HALF_COLS]
            n_im = ar * s_im + ai * s_re + buf_ref[pl.ds(r, 8), c_im:c_im + S5_HALF_COLS]
            buf_ref[pl.ds(r, 8), c_re:c_re + S5_HALF_COLS] = n_re
            buf_ref[pl.ds(r, 8), c_im:c_im + S5_HALF_COLS] = n_im
            return n_re, n_im

        init = (st_ref[:, c_re:c_re + S5_HALF_COLS], st_ref[:, c_im:c_im + S5_HALF_COLS])
        s_re, s_im = lax.fori_loop(0, S5_T, body, init, unroll=2)
        st_ref[:, c_re:c_re + S5_HALF_COLS] = s_re
        st_ref[:, c_im:c_im + S5_HALF_COLS] = s_im

    ys = [_dot(buf_ref[:, kt * half_w:(kt + 1) * half_w].astype(BF16), cmat_ref[kt])
          for kt in range(2)]
    y = jnp.concatenate(ys, axis=-1) + d_ref[...] * ut
    y = _gelu_tanh(y)
    z = _dot(y.astype(BF16), wglu_ref[...])
    out = (z[:, :S5_WIDTH] * jax.nn.sigmoid(z[:, S5_WIDTH:])).astype(BF16)
    o_ref[...] = _dot(permt_ref[...], out).astype(BF16).reshape(8, S5_T, S5_WIDTH)


def _s5(u, perm, permt, bmat, cmat, ar, ai, d, wglu):
    bsz, seq, _ = u.shape
    blk = lambda j: (0, j, 0)
    return pl.pallas_call(
        _s5_kernel,
        grid=(seq // S5_T,),
        in_specs=[
            pl.BlockSpec((bsz, S5_T, S5_WIDTH), blk),
            _const_spec(perm.shape), _const_spec(permt.shape),
            _const_spec(bmat.shape), _const_spec(cmat.shape),
            _const_spec(ar.shape), _const_spec(ai.shape),
            _const_spec(d.shape), _const_spec(wglu.shape),
        ],
        out_specs=pl.BlockSpec((bsz, S5_T, S5_WIDTH), blk),
        out_shape=jax.ShapeDtypeStruct((bsz, seq, S5_WIDTH), BF16),
        scratch_shapes=[
            pltpu.VMEM((S5_ROWS, 4 * S5_HALF_COLS), F32),
            pltpu.VMEM((8, 4 * S5_HALF_COLS), F32),
        ],
        compiler_params=pltpu.CompilerParams(
            dimension_semantics=("arbitrary",), vmem_limit_bytes=VMEM_LIMIT),
        name="s5",
    )(u, perm, permt, bmat, cmat, ar, ai, d, wglu)


def _s5_params(a_re, a_im, log_dt, b_re, b_im, c_re, c_im):
    dt = jnp.exp(log_dt)[:, None]
    x = a_re * dt
    y = a_im * dt
    ex = jnp.exp(x)
    abar_re = ex * jnp.cos(y)
    abar_im = ex * jnp.sin(y)
    m1_re = jnp.expm1(x) * jnp.cos(y) - 2.0 * jnp.sin(0.5 * y) ** 2
    den = a_re * a_re + a_im * a_im
    coef_re = (m1_re * a_re + abar_im * a_im) / den
    coef_im = (abar_im * a_re - m1_re * a_im) / den
    bb_re = coef_re[..., None] * b_re - coef_im[..., None] * b_im
    bb_im = coef_re[..., None] * b_im + coef_im[..., None] * b_re

    eye = jnp.eye(S5_HALF_GROUPS, dtype=F32)

    def pack_b(bb):
        return jnp.einsum('ab,bpc->acbp', eye, bb).reshape(256, S5_HALF_COLS)

    def pack_c(cc):
        return jnp.einsum('ab,bcp->bpac', eye, cc).reshape(S5_HALF_COLS, 256)

    bmat, cmat = [], []
    for kt in range(2):
        gs = slice(kt * S5_HALF_GROUPS, (kt + 1) * S5_HALF_GROUPS)
        bmat.append(jnp.concatenate([pack_b(bb_re[gs]), pack_b(bb_im[gs])], axis=1))
        cmat.append(jnp.concatenate([pack_c(c_re[gs]), pack_c(-c_im[gs])], axis=0))
    bmat = jnp.stack(bmat).astype(BF16)
    cmat = jnp.stack(cmat).astype(BF16)
    ar = jnp.broadcast_to(abar_re.reshape(1, -1), (8, S5_GROUPS * S5_STATE))
    ai = jnp.broadcast_to(abar_im.reshape(1, -1), (8, S5_GROUPS * S5_STATE))
    return bmat, cmat, ar, ai


def _gla_kernel(q_ref, k_ref, la_ref, v_ref, sg_ref, ltri_ref, hng_ref, o_ref, s_ref):
    @pl.when(pl.program_id(1) == 0)
    def _():
        s_ref[...] = jnp.zeros_like(s_ref)

    ltri = ltri_ref[...]
    k_ends, tots = [], []
    for c in range(GLA_NCH):
        rows = slice(c * CHUNK, (c + 1) * CHUNK)
        la_hi, la_lo = _split_bf16(la_ref[rows, :])
        cum = _dot(ltri, la_hi) + _dot(ltri, la_lo)
        tot = cum[CHUNK - 1:CHUNK, :]
        k_ends.append((k_ref[rows, :] * jnp.exp(tot - cum)).astype(BF16))
        tots.append(tot)
    tot_all = jnp.concatenate(tots + [jnp.zeros((128 - GLA_NCH, GLA_KEY), F32)], axis=0)
    dec_t = jnp.exp(tot_all).T

    for c in range(GLA_NCH):
        rows = slice(c * CHUNK, (c + 1) * CHUNK)
        for h in range(GLA_HEADS):
            ks = slice(h * GLA_DK, (h + 1) * GLA_DK)
            vs = slice(h * GLA_DV, (h + 1) * GLA_DV)
            kv = lax.dot_general(k_ends[c][:, ks], v_ref[rows, vs],
                                 (((0,), (0,)), ((), ())), preferred_element_type=F32)
            state = s_ref[h] * dec_t[ks, c:c + 1] + kv
            s_ref[h] = state
            o = _dot(q_ref[rows, ks], state.astype(BF16))
            o = _rms(o, hng_ref[:, vs]) * sg_ref[rows, vs].astype(F32)
            o_ref[rows, vs] = o.astype(BF16)


def _gla(q, k, la, v, sg, ltri, hng, bsz):
    n = q.shape[0]
    nj = n // bsz // GLA_T
    row = lambda b, j: (b * nj + j, 0)
    return pl.pallas_call(
        _gla_kernel,
        grid=(bsz, nj),
        in_specs=[
            pl.BlockSpec((GLA_T, GLA_KEY), row),
            pl.BlockSpec((GLA_T, GLA_KEY), row),
            pl.BlockSpec((GLA_T, GLA_KEY), row),
            pl.BlockSpec((GLA_T, GLA_VAL), row),
            pl.BlockSpec((GLA_T, GLA_VAL), row),
            _const_spec(ltri.shape), _const_spec(hng.shape),
        ],
        out_specs=pl.BlockSpec((GLA_T, GLA_VAL), row),
        out_shape=jax.ShapeDtypeStruct((n, GLA_VAL), BF16),
        scratch_shapes=[pltpu.VMEM((GLA_HEADS, GLA_DK, GLA_DV), F32)],
        compiler_params=pltpu.CompilerParams(
            dimension_semantics=("arbitrary", "arbitrary"), vmem_limit_bytes=VMEM_LIMIT),
        name="gla",
    )(q, k, la, v, sg, ltri, hng)


def _mix_kernel(h_ref, ys5_ref, ygla_ref, ss5_ref, sgla_ref, wbs5_ref, wbgla_ref, wout_ref, o_ref):
    mixed = (ss5_ref[...].astype(F32) * _dot(ys5_ref[...], wbs5_ref[...])
             + sgla_ref[...].astype(F32) * _dot(ygla_ref[...], wbgla_ref[...]))
    o_ref[...] = h_ref[...] + _dot(mixed.astype(BF16), wout_ref[...])


def _mix(h, ys5, ygla, ss5, sgla, wbs5, wbgla, wout):
    n = h.shape[0]
    row = lambda i: (i, 0)
    return pl.pallas_call(
        _mix_kernel,
        grid=(n // ROW_TILE,),
        in_specs=[
            pl.BlockSpec((ROW_TILE, D_MODEL), row),
            pl.BlockSpec((ROW_TILE, S5_WIDTH), row),
            pl.BlockSpec((ROW_TILE, GLA_VAL), row),
            pl.BlockSpec((ROW_TILE, D_MODEL), row),
            pl.BlockSpec((ROW_TILE, D_MODEL), row),
            _const_spec(wbs5.shape), _const_spec(wbgla.shape), _const_spec(wout.shape),
        ],
        out_specs=pl.BlockSpec((ROW_TILE, D_MODEL), row),
        out_shape=jax.ShapeDtypeStruct((n, D_MODEL), F32),
        compiler_params=pltpu.CompilerParams(
            dimension_semantics=("arbitrary",), vmem_limit_bytes=VMEM_LIMIT),
        name="mix",
    )(h, ys5, ygla, ss5, sgla, wbs5, wbgla, wout)


def _ffn_kernel(h_ref, g_ref, wg_ref, wu_ref, wd_ref, fg_ref, o_ref, *, final_norm):
    h = h_ref[...]
    hn = _rms(h, g_ref[...]).astype(BF16)
    acc = h
    for a, b in FF_CHUNKS:
        gate = _dot(hn, wg_ref[:, a:b])
        up = _dot(hn, wu_ref[:, a:b])
        act = (gate * jax.nn.sigmoid(gate) * up).astype(BF16)
        acc = acc + _dot(act, wd_ref[a:b, :])
    if final_norm:
        acc = _rms(acc, fg_ref[...])
    o_ref[...] = acc


def _ffn(h, g, wg, wu, wd, fg, final_norm):
    n = h.shape[0]
    row = lambda i: (i, 0)
    return pl.pallas_call(
        functools.partial(_ffn_kernel, final_norm=final_norm),
        grid=(n // ROW_TILE,),
        in_specs=[
            pl.BlockSpec((ROW_TILE, D_MODEL), row),
            _const_spec(g.shape), _const_spec(wg.shape), _const_spec(wu.shape),
            _const_spec(wd.shape), _const_spec(fg.shape),
        ],
        out_specs=pl.BlockSpec((ROW_TILE, D_MODEL), row),
        out_shape=jax.ShapeDtypeStruct((n, D_MODEL), F32),
        compiler_params=pltpu.CompilerParams(
            dimension_semantics=("arbitrary",), vmem_limit_bytes=VMEM_LIMIT),
        name="ffn_final" if final_norm else "ffn",
    )(h, g, wg, wu, wd, fg)


def _pack_w_in(w):
    u, q, k, v, g, a_low, gs5, ggla = jnp.split(
        w, [512, 1024, 1536, 2560, 3584, 3600, 4624], axis=-1)
    pad = jnp.zeros((D_MODEL, _ALOW_PAD - GLA_GATE_RANK), w.dtype)
    return jnp.concatenate([u, q, k, v, g, gs5, ggla, a_low, pad], axis=-1).astype(BF16)


def _perm_matrices():
    p = np.zeros((S5_ROWS, S5_ROWS), np.float32)
    t, b = np.meshgrid(np.arange(S5_T), np.arange(8), indexing='ij')
    p[(t * 8 + b).ravel(), (b * S5_T + t).ravel()] = 1.0
    return jnp.asarray(p, BF16), jnp.asarray(p.T, BF16)


def kernel(x, attn_norm_g, w_in, s5_a_re, s5_a_im, s5_log_dt, s5_b_re, s5_b_im, s5_c_re, s5_c_im,
           s5_d, s5_w_glu, gla_w_gate_up, gla_b_gate, gla_head_norm_g, w_branch_s5, w_branch_gla,
           w_out, ffn_norm_g, w_ffn_gate, w_ffn_up, w_ffn_down, final_norm_g):
    bsz, seq, _ = x.shape
    n = bsz * seq
    depth = w_in.shape[0]
    perm, permt = _perm_matrices()
    ltri = jnp.asarray(np.tril(np.ones((CHUNK, CHUNK), np.float32)), BF16)
    fg = final_norm_g.reshape(1, D_MODEL)

    h = x.reshape(n, D_MODEL)
    for l in range(depth):
        wup = jnp.concatenate(
            [gla_w_gate_up[l], jnp.zeros((_ALOW_PAD - GLA_GATE_RANK, GLA_KEY), F32)], axis=0)
        u, q, k, la, v, sg, ss5, sgla = _inproj(
            h, attn_norm_g[l].reshape(1, -1), _pack_w_in(w_in[l]), wup.astype(BF16),
            gla_b_gate[l].reshape(1, -1))
        bmat, cmat, ar, ai = _s5_params(s5_a_re[l], s5_a_im[l], s5_log_dt[l], s5_b_re[l],
                                        s5_b_im[l], s5_c_re[l], s5_c_im[l])
        ys5 = _s5(u.reshape(bsz, seq, S5_WIDTH), perm, permt, bmat, cmat, ar, ai,
                  s5_d[l].reshape(1, -1), s5_w_glu[l].astype(BF16)).reshape(n, S5_WIDTH)
        ygla = _gla(q, k, la, v, sg, ltri, gla_head_norm_g[l].reshape(1, -1), bsz)
        h = _mix(h, ys5, ygla, ss5, sgla, w_branch_s5[l].astype(BF16),
                 w_branch_gla[l].astype(BF16), w_out[l].astype(BF16))
        h = _ffn(h, ffn_norm_g[l].reshape(1, -1), w_ffn_gate[l].astype(BF16),
                 w_ffn_up[l].astype(BF16), w_ffn_down[l].astype(BF16), fg,
                 final_norm=(l == depth - 1))
    return h.reshape(bsz, seq, D_MODEL)
```

```python
import functools
import math

import numpy as np
import jax
import jax.numpy as jnp
from jax import lax
from jax.experimental import pallas as pl
from jax.experimental.pallas import tpu as pltpu

F32 = jnp.float32
BF16 = jnp.bfloat16

D_MODEL = 1024
EPS = 1e-6
CHUNK = 64

S5_WIDTH = 512
S5_GROUP = 16
S5_GROUPS = 32
S5_STATE = 64
S5_HALF_GROUPS = S5_GROUPS // 2
S5_HALF_COLS = S5_HALF_GROUPS * S5_STATE

GLA_HEADS = 4
GLA_KEY = 512
GLA_VAL = 1024
GLA_DK = 128
GLA_DV = 256
GLA_GATE_RANK = 16
GLA_GATE_TEMP = 16.0

D_FF = 2816
FF_CHUNKS = ((0, 1024), (1024, 2048), (2048, 2816))

_OFF_U, _OFF_Q, _OFF_K, _OFF_V, _OFF_G, _OFF_GS5, _OFF_GGLA, _OFF_ALOW, _W_IN_COLS = (
    0, 512, 1024, 1536, 2560, 3584, 4608, 5632, 5760)
_ALOW_PAD = _W_IN_COLS - _OFF_ALOW

ROW_TILE = 512
S5_T = 64
S5_ROWS = 8 * S5_T
GLA_T = 512
GLA_NCH = GLA_T // CHUNK

VMEM_LIMIT = 56 * 1024 * 1024


def _dot(a, b):
    return jnp.dot(a, b, preferred_element_type=F32)


def _split_bf16(x):
    hi = x.astype(BF16)
    lo = (x - hi.astype(F32)).astype(BF16)
    return hi, lo


def _inv_rms(x):
    return lax.rsqrt(jnp.mean(x * x, axis=-1, keepdims=True) + EPS)


def _const_spec(shape):
    zeros = (0,) * len(shape)
    return pl.BlockSpec(shape, lambda *_: zeros, pipeline_mode=pl.Buffered(1))


def _layer_spec(arr, layer):
    idx = (layer,) + (0,) * (arr.ndim - 1)
    return pl.BlockSpec((None,) + arr.shape[1:], lambda *_: idx, pipeline_mode=pl.Buffered(1))


def _params(*semantics):
    return pltpu.CompilerParams(dimension_semantics=semantics, vmem_limit_bytes=VMEM_LIMIT)


def _inproj_kernel(x_ref, g_ref, w_ref, wup_ref, bg_ref,
                   u_ref, q_ref, k_ref, la_ref, v_ref, sg_ref, ss5_ref, sgla_ref):
    x = x_ref[...]
    xg = (x * g_ref[...]).astype(BF16)
    inv = _inv_rms(x)

    def seg(a, b):
        return _dot(xg, w_ref[:, a:b]) * inv

    a_low = seg(_OFF_ALOW, _W_IN_COLS)
    z = _dot(a_low.astype(BF16), wup_ref[...]) + bg_ref[...]
    log_sig = jnp.minimum(z, 0.0) - jnp.log1p(jnp.exp(-jnp.abs(z)))
    la_ref[...] = log_sig * (1.0 / GLA_GATE_TEMP)
    u_ref[...] = seg(_OFF_U, _OFF_Q)
    q_ref[...] = (seg(_OFF_Q, _OFF_K) * (GLA_DK ** -0.5)).astype(BF16)
    k_ref[...] = seg(_OFF_K, _OFF_V)
    v_ref[...] = seg(_OFF_V, _OFF_G).astype(BF16)
    g = seg(_OFF_G, _OFF_GS5)
    sg_ref[...] = (g * jax.nn.sigmoid(g)).astype(BF16)
    ss5_ref[...] = jax.nn.sigmoid(seg(_OFF_GS5, _OFF_GGLA)).astype(BF16)
    sgla_ref[...] = jax.nn.sigmoid(seg(_OFF_GGLA, _OFF_ALOW)).astype(BF16)


def _inproj(h, layer, g, w, wup, bg):
    n = h.shape[0]
    row = lambda i: (i, 0)
    outs = [
        (S5_WIDTH, F32), (GLA_KEY, BF16), (GLA_KEY, F32), (GLA_KEY, F32),
        (GLA_VAL, BF16), (GLA_VAL, BF16), (D_MODEL, BF16), (D_MODEL, BF16)]
    return pl.pallas_call(
        _inproj_kernel,
        grid=(n // ROW_TILE,),
        in_specs=[
            pl.BlockSpec((ROW_TILE, D_MODEL), row),
            _layer_spec(g, layer), _layer_spec(w, layer),
            _layer_spec(wup, layer), _layer_spec(bg, layer),
        ],
        out_specs=[pl.BlockSpec((ROW_TILE, c), row) for c, _ in outs],
        out_shape=[jax.ShapeDtypeStruct((n, c), dt) for c, dt in outs],
        compiler_params=_params("arbitrary"),
        name="inproj",
    )(h, g, w, wup, bg)


def _gelu_tanh(x):
    c = math.sqrt(2.0 / math.pi)
    return 0.5 * x * (1.0 + jnp.tanh(c * (x + 0.044715 * (x * x * x))))


def _s5_kernel(u_ref, perm_ref, permt_ref, bmat_ref, cmat_ref, ar_ref, ai_ref, d_ref, wglu_ref,
               o_ref, buf_ref, st_ref):
    @pl.when(pl.program_id(0) == 0)
    def _():
        st_ref[...] = jnp.zeros_like(st_ref)

    u_bf = u_ref[...].reshape(S5_ROWS, S5_WIDTH).astype(BF16)
    ut = _dot(perm_ref[...], u_bf)
    ut_bf = ut.astype(BF16)

    half_w = 2 * S5_HALF_COLS
    for kt in range(2):
        buf_ref[:, kt * half_w:(kt + 1) * half_w] = _dot(
            ut_bf[:, kt * 256:(kt + 1) * 256], bmat_ref[kt])

    for kt in range(2):
        c_re = kt * half_w
        c_im = c_re + S5_HALF_COLS
        ar = ar_ref[:, kt * S5_HALF_COLS:(kt + 1) * S5_HALF_COLS]
        ai = ai_ref[:, kt * S5_HALF_COLS:(kt + 1) * S5_HALF_COLS]

        def body(t, carry, c_re=c_re, c_im=c_im, ar=ar, ai=ai):
            s_re, s_im = carry
            r = pl.multiple_of(t * 8, 8)
            n_re = ar * s_re - ai * s_im + buf_ref[pl.ds(r, 8), c_re:c_re + S5_HALF_COLS]
            n_im = ar * s_im + ai * s_re + buf_ref[pl.ds(r, 8), c_im:c_im + S5_HALF_COLS]
            buf_ref[pl.ds(r, 8), c_re:c_re + S5_HALF_COLS] = n_re
            buf_ref[pl.ds(r, 8), c_im:c_im + S5_HALF_COLS] = n_im
            return n_re, n_im

        init = (st_ref[:, c_re:c_re + S5_HALF_COLS], st_ref[:, c_im:c_im + S5_HALF_COLS])
        s_re, s_im = lax.fori_loop(0, S5_T, body, init, unroll=2)
        st_ref[:, c_re:c_re + S5_HALF_COLS] = s_re
        st_ref[:, c_im:c_im + S5_HALF_COLS] = s_im

    ys = [_dot(buf_ref[:, kt * half_w:(kt + 1) * half_w].astype(BF16), cmat_ref[kt])
          for kt in range(2)]
    y = jnp.concatenate(ys, axis=-1) + d_ref[...] * ut
    y = _gelu_tanh(y)
    z = _dot(y.astype(BF16), wglu_ref[...])
    out = (z[:, :S5_WIDTH] * jax.nn.sigmoid(z[:, S5_WIDTH:])).astype(BF16)
    o_ref[...] = _dot(permt_ref[...], out).astype(BF16).reshape(8, S5_T, S5_WIDTH)


def _s5(u, layer, perm, permt, bmat, cmat, ar, ai, d, wglu):
    bsz, seq, _ = u.shape
    blk = lambda j: (0, j, 0)
    return pl.pallas_call(
        _s5_kernel,
        grid=(seq // S5_T,),
        in_specs=[
            pl.BlockSpec((bsz, S5_T, S5_WIDTH), blk),
            _const_spec(perm.shape), _const_spec(permt.shape),
            _layer_spec(bmat, layer), _layer_spec(cmat, layer),
            _layer_spec(ar, layer), _layer_spec(ai, layer),
            _layer_spec(d, layer), _layer_spec(wglu, layer),
        ],
        out_specs=pl.BlockSpec((bsz, S5_T, S5_WIDTH), blk),
        out_shape=jax.ShapeDtypeStruct((bsz, seq, S5_WIDTH), BF16),
        scratch_shapes=[
            pltpu.VMEM((S5_ROWS, 4 * S5_HALF_COLS), F32),
            pltpu.VMEM((8, 4 * S5_HALF_COLS), F32),
        ],
        compiler_params=_params("arbitrary"),
        name="s5",
    )(u, perm, permt, bmat, cmat, ar, ai, d, wglu)


def _s5_params(a_re, a_im, log_dt, b_re, b_im, c_re, c_im):
    depth = a_re.shape[0]
    dt = jnp.exp(log_dt)[..., None]
    x = a_re * dt
    y = a_im * dt
    ex = jnp.exp(x)
    abar_re = ex * jnp.cos(y)
    abar_im = ex * jnp.sin(y)
    m1_re = jnp.expm1(x) * jnp.cos(y) - 2.0 * jnp.sin(0.5 * y) ** 2
    den = a_re * a_re + a_im * a_im
    coef_re = (m1_re * a_re + abar_im * a_im) / den
    coef_im = (abar_im * a_re - m1_re * a_im) / den
    bb_re = coef_re[..., None] * b_re - coef_im[..., None] * b_im
    bb_im = coef_re[..., None] * b_im + coef_im[..., None] * b_re

    eye = jnp.eye(S5_HALF_GROUPS, dtype=F32)

    def halves(t):
        return t.reshape((depth, 2, S5_HALF_GROUPS) + t.shape[2:])

    def pack_b(bb):
        return jnp.einsum('ab,lkbpc->lkacbp', eye, halves(bb)).reshape(
            depth, 2, 256, S5_HALF_COLS)

    def pack_c(cc):
        return jnp.einsum('ab,lkbcp->lkbpac', eye, halves(cc)).reshape(
            depth, 2, S5_HALF_COLS, 256)

    bmat = jnp.concatenate([pack_b(bb_re), pack_b(bb_im)], axis=3).astype(BF16)
    cmat = jnp.concatenate([pack_c(c_re), pack_c(-c_im)], axis=2).astype(BF16)
    cols = S5_GROUPS * S5_STATE
    ar = jnp.broadcast_to(abar_re.reshape(depth, 1, cols), (depth, 8, cols))
    ai = jnp.broadcast_to(abar_im.reshape(depth, 1, cols), (depth, 8, cols))
    return bmat, cmat, ar, ai


def _gla_kernel(q_ref, k_ref, la_ref, v_ref, sg_ref, ltri_ref, hng_ref, o_ref,
                s_ref, kv_ref, sb_ref):
    @pl.when(pl.program_id(1) == 0)
    def _():
        s_ref[...] = jnp.zeros_like(s_ref)

    ltri = ltri_ref[...]
    tots = []
    for c in range(GLA_NCH):
        rows = slice(c * CHUNK, (c + 1) * CHUNK)
        la_hi, la_lo = _split_bf16(la_ref[rows, :])
        cum = _dot(ltri, la_hi) + _dot(ltri, la_lo)
        tot = cum[CHUNK - 1:CHUNK, :]
        tots.append(tot)
        k_end = (k_ref[rows, :] * jnp.exp(tot - cum)).astype(BF16)
        for h in range(GLA_HEADS):
            kv_ref[c, h] = lax.dot_general(
                k_end[:, h * GLA_DK:(h + 1) * GLA_DK], v_ref[rows, h * GLA_DV:(h + 1) * GLA_DV],
                (((0,), (0,)), ((), ())), preferred_element_type=F32)

    tot_all = jnp.concatenate(tots + [jnp.zeros((128 - GLA_NCH, GLA_KEY), F32)], axis=0)
    dec_t = jnp.exp(tot_all).T

    states = [s_ref[h] for h in range(GLA_HEADS)]
    for c in range(GLA_NCH):
        for h in range(GLA_HEADS):
            states[h] = states[h] * dec_t[h * GLA_DK:(h + 1) * GLA_DK, c:c + 1] + kv_ref[c, h]
            sb_ref[c, h] = states[h].astype(BF16)
    for h in range(GLA_HEADS):
        s_ref[h] = states[h]

    for c in range(GLA_NCH):
        rows = slice(c * CHUNK, (c + 1) * CHUNK)
        for h in range(GLA_HEADS):
            vs = slice(h * GLA_DV, (h + 1) * GLA_DV)
            o = _dot(q_ref[rows, h * GLA_DK:(h + 1) * GLA_DK], sb_ref[c, h])
            o = o * _inv_rms(o) * hng_ref[:, vs] * sg_ref[rows, vs].astype(F32)
            o_ref[rows, vs] = o.astype(BF16)


def _gla(q, k, la, v, sg, layer, ltri, hng, bsz):
    n = q.shape[0]
    nj = n // bsz // GLA_T
    row = lambda b, j: (b * nj + j, 0)
    return pl.pallas_call(
        _gla_kernel,
        grid=(bsz, nj),
        in_specs=[
            pl.BlockSpec((GLA_T, GLA_KEY), row),
            pl.BlockSpec((GLA_T, GLA_KEY), row),
            pl.BlockSpec((GLA_T, GLA_KEY), row),
            pl.BlockSpec((GLA_T, GLA_VAL), row),
            pl.BlockSpec((GLA_T, GLA_VAL), row),
            _const_spec(ltri.shape), _layer_spec(hng, layer),
        ],
        out_specs=pl.BlockSpec((GLA_T, GLA_VAL), row),
        out_shape=jax.ShapeDtypeStruct((n, GLA_VAL), BF16),
        scratch_shapes=[
            pltpu.VMEM((GLA_HEADS, GLA_DK, GLA_DV), F32),
            pltpu.VMEM((GLA_NCH, GLA_HEADS, GLA_DK, GLA_DV), F32),
            pltpu.VMEM((GLA_NCH, GLA_HEADS, GLA_DK, GLA_DV), BF16),
        ],
        compiler_params=_params("arbitrary", "arbitrary"),
        name="gla",
    )(q, k, la, v, sg, ltri, hng)


def _mix_ffn_kernel(h_ref, ys5_ref, ygla_ref, ss5_ref, sgla_ref, wbs5_ref, wbgla_ref, wout_ref,
                    g_ref, wg_ref, wu_ref, wd_ref, fg_ref, o_ref, *, final_norm):
    mixed = (ss5_ref[...].astype(F32) * _dot(ys5_ref[...], wbs5_ref[...])
             + sgla_ref[...].astype(F32) * _dot(ygla_ref[...], wbgla_ref[...]))
    h = h_ref[...] + _dot(mixed.astype(BF16), wout_ref[...])
    hg = (h * g_ref[...]).astype(BF16)
    inv = _inv_rms(h)
    acc = h
    for a, b in FF_CHUNKS:
        gate = _dot(hg, wg_ref[:, a:b]) * inv
        up = _dot(hg, wu_ref[:, a:b]) * inv
        act = (gate * jax.nn.sigmoid(gate) * up).astype(BF16)
        acc = acc + _dot(act, wd_ref[a:b, :])
    if final_norm:
        acc = acc * _inv_rms(acc) * fg_ref[...]
    o_ref[...] = acc


def _mix_ffn(h, ys5, ygla, ss5, sgla, layer, wbs5, wbgla, wout, g, wg, wu, wd, fg, final_norm):
    n = h.shape[0]
    row = lambda i: (i, 0)
    return pl.pallas_call(
        functools.partial(_mix_ffn_kernel, final_norm=final_norm),
        grid=(n // ROW_TILE,),
        in_specs=[
            pl.BlockSpec((ROW_TILE, D_MODEL), row),
            pl.BlockSpec((ROW_TILE, S5_WIDTH), row),
            pl.BlockSpec((ROW_TILE, GLA_VAL), row),
            pl.BlockSpec((ROW_TILE, D_MODEL), row),
            pl.BlockSpec((ROW_TILE, D_MODEL), row),
            _layer_spec(wbs5, layer), _layer_spec(wbgla, layer), _layer_spec(wout, layer),
            _layer_spec(g, layer), _layer_spec(wg, layer), _layer_spec(wu, layer),
            _layer_spec(wd, layer), _const_spec(fg.shape),
        ],
        out_specs=pl.BlockSpec((ROW_TILE, D_MODEL), row),
        out_shape=jax.ShapeDtypeStruct((n, D_MODEL), F32),
        compiler_params=_params("arbitrary"),
        name="mix_ffn_final" if final_norm else "mix_ffn",
    )(h, ys5, ygla, ss5, sgla, wbs5, wbgla, wout, g, wg, wu, wd, fg)


def _pack_w_in(w):
    u, q, k, v, g, a_low, gs5, ggla = jnp.split(
        w, [512, 1024, 1536, 2560, 3584, 3600, 4624], axis=-1)
    pad = jnp.zeros(w.shape[:-1] + (_ALOW_PAD - GLA_GATE_RANK,), w.dtype)
    return jnp.concatenate([u, q, k, v, g, gs5, ggla, a_low, pad], axis=-1).astype(BF16)


def _perm_matrices():
    p = np.zeros((S5_ROWS, S5_ROWS), np.float32)
    t, b = np.meshgrid(np.arange(S5_T), np.arange(8), indexing='ij')
    p[(t * 8 + b).ravel(), (b * S5_T + t).ravel()] = 1.0
    return jnp.asarray(p, BF16), jnp.asarray(p.T, BF16)


def kernel(x, attn_norm_g, w_in, s5_a_re, s5_a_im, s5_log_dt, s5_b_re, s5_b_im, s5_c_re, s5_c_im,
           s5_d, s5_w_glu, gla_w_gate_up, gla_b_gate, gla_head_norm_g, w_branch_s5, w_branch_gla,
           w_out, ffn_norm_g, w_ffn_gate, w_ffn_up, w_ffn_down, final_norm_g):
    bsz, seq, _ = x.shape
    n = bsz * seq
    depth = w_in.shape[0]
    perm, permt = _perm_matrices()
    ltri = jnp.asarray(np.tril(np.ones((CHUNK, CHUNK), np.float32)), BF16)
    fg = final_norm_g.reshape(1, D_MODEL)

    vec = lambda t: t.reshape(depth, 1, -1)
    w_in_p = _pack_w_in(w_in)
    wup = jnp.concatenate(
        [gla_w_gate_up, jnp.zeros((depth, _ALOW_PAD - GLA_GATE_RANK, GLA_KEY), F32)],
        axis=1).astype(BF16)
    bmat, cmat, ar, ai = _s5_params(s5_a_re, s5_a_im, s5_log_dt, s5_b_re, s5_b_im,
                                    s5_c_re, s5_c_im)
    wglu = s5_w_glu.astype(BF16)
    wbs5, wbgla, wout = (w_branch_s5.astype(BF16), w_branch_gla.astype(BF16),
                         w_out.astype(BF16))
    wg, wu, wd = w_ffn_gate.astype(BF16), w_ffn_up.astype(BF16), w_ffn_down.astype(BF16)
    attn_g, ffn_g, bgate, hng, s5d = (vec(attn_norm_g), vec(ffn_norm_g), vec(gla_b_gate),
                                      vec(gla_head_norm_g), vec(s5_d))

    h = x.reshape(n, D_MODEL)
    for l in range(depth):
        u, q, k, la, v, sg, ss5, sgla = _inproj(h, l, attn_g, w_in_p, wup, bgate)
        ys5 = _s5(u.reshape(bsz, seq, S5_WIDTH), l, perm, permt, bmat, cmat, ar, ai,
                  s5d, wglu).reshape(n, S5_WIDTH)
        ygla = _gla(q, k, la, v, sg, l, ltri, hng, bsz)
        h = _mix_ffn(h, ys5, ygla, ss5, sgla, l, wbs5, wbgla, wout, ffn_g, wg, wu, wd, fg,
                     final_norm=(l == depth - 1))
    return h.reshape(bsz, seq, D_MODEL)
```

```python
import functools
import math

import numpy as np
import jax
import jax.numpy as jnp
from jax import lax
from jax.experimental import pallas as pl
from jax.experimental.pallas import tpu as pltpu

F32 = jnp.float32
BF16 = jnp.bfloat16

D_MODEL = 1024
EPS = 1e-6
CHUNK = 64

S5_WIDTH = 512
S5_GROUP = 16
S5_GROUPS = 32
S5_STATE = 64
S5_HALF_GROUPS = S5_GROUPS // 2
S5_HALF_COLS = S5_HALF_GROUPS * S5_STATE

GLA_HEADS = 4
GLA_KEY = 512
GLA_VAL = 1024
GLA_DK = 128
GLA_DV = 256
GLA_GATE_RANK = 16
GLA_GATE_TEMP = 16.0

D_FF = 2816
FF_CHUNKS = ((0, 1024), (1024, 2048), (2048, 2816))

_OFF_U, _OFF_Q, _OFF_V, _OFF_G, _OFF_GS5, _OFF_GGLA, _W_IN_COLS = (
    0, 512, 1024, 2048, 3072, 4096, 5120)
_ALOW_PAD = 128

ROW_TILE = 512
S5_T = 64
S5_ROWS = 8 * S5_T
GLA_T = 512
GLA_NCH = GLA_T // CHUNK
GLA_CUM_T = 256

VMEM_LIMIT = 56 * 1024 * 1024


def _dot(a, b):
    return jnp.dot(a, b, preferred_element_type=F32)


def _split_bf16(x):
    hi = x.astype(BF16)
    lo = (x - hi.astype(F32)).astype(BF16)
    return hi, lo


def _inv_rms(x):
    return lax.rsqrt(jnp.mean(x * x, axis=-1, keepdims=True) + EPS)


def _const_spec(shape):
    zeros = (0,) * len(shape)
    return pl.BlockSpec(shape, lambda *_: zeros, pipeline_mode=pl.Buffered(1))


def _layer_spec(arr, layer):
    idx = (layer,) + (0,) * (arr.ndim - 1)
    return pl.BlockSpec((None,) + arr.shape[1:], lambda *_: idx, pipeline_mode=pl.Buffered(1))


def _params(*semantics):
    return pltpu.CompilerParams(dimension_semantics=semantics, vmem_limit_bytes=VMEM_LIMIT)


def _dot_nt(a, b):
    return lax.dot_general(a, b, (((1,), (1,)), ((), ())), preferred_element_type=F32)


def _inproj_kernel(x_ref, g_ref, w_ref, wkat_ref, wupt_ref, bg_ref,
                   u_ref, q_ref, kt_ref, lat_ref, v_ref, sg_ref, ss5_ref, sgla_ref):
    x = x_ref[...]
    xg = (x * g_ref[...]).astype(BF16)
    inv = _inv_rms(x)
    inv_t = jnp.broadcast_to(inv, (ROW_TILE, 128)).T[0:1, :]

    def seg(a, b):
        return _dot(xg, w_ref[:, a:b]) * inv

    ka_t = _dot_nt(wkat_ref[...], xg) * inv_t
    kt_ref[...] = ka_t[:GLA_KEY, :]
    z_t = _dot(wupt_ref[...], ka_t[GLA_KEY:, :].astype(BF16)) + bg_ref[...]
    log_sig = jnp.minimum(z_t, 0.0) - jnp.log1p(jnp.exp(-jnp.abs(z_t)))
    lat_ref[...] = log_sig * (1.0 / GLA_GATE_TEMP)
    u_ref[...] = seg(_OFF_U, _OFF_Q)
    q_ref[...] = (seg(_OFF_Q, _OFF_V) * (GLA_DK ** -0.5)).astype(BF16)
    v_ref[...] = seg(_OFF_V, _OFF_G).astype(BF16)
    g = seg(_OFF_G, _OFF_GS5)
    sg_ref[...] = (g * jax.nn.sigmoid(g)).astype(BF16)
    ss5_ref[...] = jax.nn.sigmoid(seg(_OFF_GS5, _OFF_GGLA)).astype(BF16)
    sgla_ref[...] = jax.nn.sigmoid(seg(_OFF_GGLA, _W_IN_COLS)).astype(BF16)


def _inproj(h, layer, g, w, wkat, wupt, bg):
    n = h.shape[0]
    row = lambda i: (i, 0)
    col = lambda i: (0, i)
    outs = [
        ((n, S5_WIDTH), F32, (ROW_TILE, S5_WIDTH), row),
        ((n, GLA_KEY), BF16, (ROW_TILE, GLA_KEY), row),
        ((GLA_KEY, n), F32, (GLA_KEY, ROW_TILE), col),
        ((GLA_KEY, n), F32, (GLA_KEY, ROW_TILE), col),
        ((n, GLA_VAL), BF16, (ROW_TILE, GLA_VAL), row),
        ((n, GLA_VAL), BF16, (ROW_TILE, GLA_VAL), row),
        ((n, D_MODEL), BF16, (ROW_TILE, D_MODEL), row),
        ((n, D_MODEL), BF16, (ROW_TILE, D_MODEL), row)]
    return pl.pallas_call(
        _inproj_kernel,
        grid=(n // ROW_TILE,),
        in_specs=[
            pl.BlockSpec((ROW_TILE, D_MODEL), row),
            _layer_spec(g, layer), _layer_spec(w, layer), _layer_spec(wkat, layer),
            _layer_spec(wupt, layer), _layer_spec(bg, layer),
        ],
        out_specs=[pl.BlockSpec(blk, imap) for _, _, blk, imap in outs],
        out_shape=[jax.ShapeDtypeStruct(shape, dt) for shape, dt, _, _ in outs],
        compiler_params=_params("arbitrary"),
        name="inproj",
    )(h, g, w, wkat, wupt, bg)


def _gelu_tanh(x):
    c = math.sqrt(2.0 / math.pi)
    return 0.5 * x * (1.0 + jnp.tanh(c * (x + 0.044715 * (x * x * x))))


def _s5_kernel(u_ref, perm_ref, permt_ref, bmat_ref, cmat_ref, ar_ref, ai_ref, d_ref, wglu_ref,
               o_ref, buf_ref, st_ref):
    @pl.when(pl.program_id(0) == 0)
    def _():
        st_ref[...] = jnp.zeros_like(st_ref)

    u_bf = u_ref[...].reshape(S5_ROWS, S5_WIDTH).astype(BF16)
    ut = _dot(perm_ref[...], u_bf)
    ut_bf = ut.astype(BF16)

    half_w = 2 * S5_HALF_COLS
    for kt in range(2):
        buf_ref[:, kt * half_w:(kt + 1) * half_w] = _dot(
            ut_bf[:, kt * 256:(kt + 1) * 256], bmat_ref[kt])

    for kt in range(2):
        c_re = kt * half_w
        c_im = c_re + S5_HALF_COLS
        re_cols = slice(c_re, c_re + S5_HALF_COLS)
        im_cols = slice(c_im, c_im + S5_HALF_COLS)
        ar = ar_ref[:, kt * S5_HALF_COLS:(kt + 1) * S5_HALF_COLS]
        ai = ai_ref[:, kt * S5_HALF_COLS:(kt + 1) * S5_HALF_COLS]
        s_re, s_im = st_ref[:, re_cols], st_ref[:, im_cols]
        for t in range(S5_T):
            rows = slice(t * 8, (t + 1) * 8)
            s_re, s_im = (ar * s_re - ai * s_im + buf_ref[rows, re_cols],
                          ar * s_im + ai * s_re + buf_ref[rows, im_cols])
            buf_ref[rows, re_cols] = s_re
            buf_ref[rows, im_cols] = s_im
        st_ref[:, re_cols] = s_re
        st_ref[:, im_cols] = s_im

    ys = [_dot(buf_ref[:, kt * half_w:(kt + 1) * half_w].astype(BF16), cmat_ref[kt])
          for kt in range(2)]
    y = jnp.concatenate(ys, axis=-1) + d_ref[...] * ut
    y = _gelu_tanh(y)
    z = _dot(y.astype(BF16), wglu_ref[...])
    out = (z[:, :S5_WIDTH] * jax.nn.sigmoid(z[:, S5_WIDTH:])).astype(BF16)
    o_ref[...] = _dot(permt_ref[...], out).astype(BF16).reshape(8, S5_T, S5_WIDTH)


def _s5(u, layer, perm, permt, bmat, cmat, ar, ai, d, wglu):
    bsz, seq, _ = u.shape
    blk = lambda j: (0, j, 0)
    return pl.pallas_call(
        _s5_kernel,
        grid=(seq // S5_T,),
        in_specs=[
            pl.BlockSpec((bsz, S5_T, S5_WIDTH), blk),
            _const_spec(perm.shape), _const_spec(permt.shape),
            _layer_spec(bmat, layer), _layer_spec(cmat, layer),
            _layer_spec(ar, layer), _layer_spec(ai, layer),
            _layer_spec(d, layer), _layer_spec(wglu, layer),
        ],
        out_specs=pl.BlockSpec((bsz, S5_T, S5_WIDTH), blk),
        out_shape=jax.ShapeDtypeStruct((bsz, seq, S5_WIDTH), BF16),
        scratch_shapes=[
            pltpu.VMEM((S5_ROWS, 4 * S5_HALF_COLS), F32),
            pltpu.VMEM((8, 4 * S5_HALF_COLS), F32),
        ],
        compiler_params=_params("arbitrary"),
        name="s5",
    )(u, perm, permt, bmat, cmat, ar, ai, d, wglu)


def _s5_params(a_re, a_im, log_dt, b_re, b_im, c_re, c_im):
    depth = a_re.shape[0]
    dt = jnp.exp(log_dt)[..., None]
    x = a_re * dt
    y = a_im * dt
    ex = jnp.exp(x)
    abar_re = ex * jnp.cos(y)
    abar_im = ex * jnp.sin(y)
    m1_re = jnp.expm1(x) * jnp.cos(y) - 2.0 * jnp.sin(0.5 * y) ** 2
    den = a_re * a_re + a_im * a_im
    coef_re = (m1_re * a_re + abar_im * a_im) / den
    coef_im = (abar_im * a_re - m1_re * a_im) / den
    bb_re = coef_re[..., None] * b_re - coef_im[..., None] * b_im
    bb_im = coef_re[..., None] * b_im + coef_im[..., None] * b_re

    eye = jnp.eye(S5_HALF_GROUPS, dtype=F32)

    def halves(t):
        return t.reshape((depth, 2, S5_HALF_GROUPS) + t.shape[2:])

    def pack_b(bb):
        return jnp.einsum('ab,lkbpc->lkacbp', eye, halves(bb)).reshape(
            depth, 2, 256, S5_HALF_COLS)

    def pack_c(cc):
        return jnp.einsum('ab,lkbcp->lkbpac', eye, halves(cc)).reshape(
            depth, 2, S5_HALF_COLS, 256)

    bmat = jnp.concatenate([pack_b(bb_re), pack_b(bb_im)], axis=3).astype(BF16)
    cmat = jnp.concatenate([pack_c(c_re), pack_c(-c_im)], axis=2).astype(BF16)
    cols = S5_GROUPS * S5_STATE
    ar = jnp.broadcast_to(abar_re.reshape(depth, 1, cols), (depth, 8, cols))
    ai = jnp.broadcast_to(abar_im.reshape(depth, 1, cols), (depth, 8, cols))
    return bmat, cmat, ar, ai


def _gla_kernel(q_ref, kt_ref, lat_ref, v_ref, sg_ref, later_ref, hng_ref, o_ref,
                s_ref, kv_ref, sb_ref):
    @pl.when(pl.program_id(1) == 0)
    def _():
        s_ref[...] = jnp.zeros_like(s_ref)

    later = later_ref[...]
    decays = []
    for grp in range(GLA_T // GLA_CUM_T):
        cols = slice(grp * GLA_CUM_T, (grp + 1) * GLA_CUM_T)
        la = lat_ref[:, cols]
        la_hi, la_lo = _split_bf16(la)
        rest = _dot(la_hi, later) + _dot(la_lo, later)
        k_end_t = (kt_ref[:, cols] * jnp.exp(rest)).astype(BF16)
        for cc in range(GLA_CUM_T // CHUNK):
            c = grp * (GLA_CUM_T // CHUNK) + cc
            first = slice(cc * CHUNK, cc * CHUNK + 1)
            rows = slice(c * CHUNK, (c + 1) * CHUNK)
            decays.append(jnp.exp(rest[:, first] + la[:, first]))
            for h in range(GLA_HEADS):
                kv_ref[c, h] = _dot(
                    k_end_t[h * GLA_DK:(h + 1) * GLA_DK, cc * CHUNK:(cc + 1) * CHUNK],
                    v_ref[rows, h * GLA_DV:(h + 1) * GLA_DV])

    states = [s_ref[h] for h in range(GLA_HEADS)]
    for c in range(GLA_NCH):
        for h in range(GLA_HEADS):
            states[h] = states[h] * decays[c][h * GLA_DK:(h + 1) * GLA_DK, :] + kv_ref[c, h]
            sb_ref[c, h] = states[h].astype(BF16)
    for h in range(GLA_HEADS):
        s_ref[h] = states[h]

    for c in range(GLA_NCH):
        rows = slice(c * CHUNK, (c + 1) * CHUNK)
        for h in range(GLA_HEADS):
            vs = slice(h * GLA_DV, (h + 1) * GLA_DV)
            o = _dot(q_ref[rows, h * GLA_DK:(h + 1) * GLA_DK], sb_ref[c, h])
            o = o * _inv_rms(o) * hng_ref[:, vs] * sg_ref[rows, vs].astype(F32)
            o_ref[rows, vs] = o.astype(BF16)


def _gla(q, kt, lat, v, sg, layer, later, hng, bsz):
    n = q.shape[0]
    nj = n // bsz // GLA_T
    row = lambda b, j: (b * nj + j, 0)
    col = lambda b, j: (0, b * nj + j)
    return pl.pallas_call(
        _gla_kernel,
        grid=(bsz, nj),
        in_specs=[
            pl.BlockSpec((GLA_T, GLA_KEY), row),
            pl.BlockSpec((GLA_KEY, GLA_T), col),
            pl.BlockSpec((GLA_KEY, GLA_T), col),
            pl.BlockSpec((GLA_T, GLA_VAL), row),
            pl.BlockSpec((GLA_T, GLA_VAL), row),
            _const_spec(later.shape), _layer_spec(hng, layer),
        ],
        out_specs=pl.BlockSpec((GLA_T, GLA_VAL), row),
        out_shape=jax.ShapeDtypeStruct((n, GLA_VAL), BF16),
        scratch_shapes=[
            pltpu.VMEM((GLA_HEADS, GLA_DK, GLA_DV), F32),
            pltpu.VMEM((GLA_NCH, GLA_HEADS, GLA_DK, GLA_DV), F32),
            pltpu.VMEM((GLA_NCH, GLA_HEADS, GLA_DK, GLA_DV), BF16),
        ],
        compiler_params=_params("arbitrary", "arbitrary"),
        name="gla",
    )(q, kt, lat, v, sg, later, hng)


def _mix_ffn_kernel(h_ref, ys5_ref, ygla_ref, ss5_ref, sgla_ref, wbs5_ref, wbgla_ref, wout_ref,
                    g_ref, wg_ref, wu_ref, wd_ref, fg_ref, o_ref, *, final_norm):
    mixed = (ss5_ref[...].astype(F32) * _dot(ys5_ref[...], wbs5_ref[...])
             + sgla_ref[...].astype(F32) * _dot(ygla_ref[...], wbgla_ref[...]))
    h = h_ref[...] + _dot(mixed.astype(BF16), wout_ref[...])
    hg = (h * g_ref[...]).astype(BF16)
    inv = _inv_rms(h)
    acc = h
    for a, b in FF_CHUNKS:
        gate = _dot(hg, wg_ref[:, a:b]) * inv
        up = _dot(hg, wu_ref[:, a:b]) * inv
        act = (gate * jax.nn.sigmoid(gate) * up).astype(BF16)
        acc = acc + _dot(act, wd_ref[a:b, :])
    if final_norm:
        acc = acc * _inv_rms(acc) * fg_ref[...]
    o_ref[...] = acc


def _mix_ffn(h, ys5, ygla, ss5, sgla, layer, wbs5, wbgla, wout, g, wg, wu, wd, fg, final_norm):
    n = h.shape[0]
    row = lambda i: (i, 0)
    return pl.pallas_call(
        functools.partial(_mix_ffn_kernel, final_norm=final_norm),
        grid=(n // ROW_TILE,),
        in_specs=[
            pl.BlockSpec((ROW_TILE, D_MODEL), row),
            pl.BlockSpec((ROW_TILE, S5_WIDTH), row),
            pl.BlockSpec((ROW_TILE, GLA_VAL), row),
            pl.BlockSpec((ROW_TILE, D_MODEL), row),
            pl.BlockSpec((ROW_TILE, D_MODEL), row),
            _layer_spec(wbs5, layer), _layer_spec(wbgla, layer), _layer_spec(wout, layer),
            _layer_spec(g, layer), _layer_spec(wg, layer), _layer_spec(wu, layer),
            _layer_spec(wd, layer), _const_spec(fg.shape),
        ],
        out_specs=pl.BlockSpec((ROW_TILE, D_MODEL), row),
        out_shape=jax.ShapeDtypeStruct((n, D_MODEL), F32),
        compiler_params=_params("arbitrary"),
        name="mix_ffn_final" if final_norm else "mix_ffn",
    )(h, ys5, ygla, ss5, sgla, wbs5, wbgla, wout, g, wg, wu, wd, fg)


def _pack_w_in(w):
    u, q, k, v, g, a_low, gs5, ggla = jnp.split(
        w, [512, 1024, 1536, 2560, 3584, 3600, 4624], axis=-1)
    main = jnp.concatenate([u, q, v, g, gs5, ggla], axis=-1).astype(BF16)
    a_low = jnp.pad(a_low, ((0, 0), (0, 0), (0, _ALOW_PAD - GLA_GATE_RANK)))
    wkat = jnp.swapaxes(jnp.concatenate([k, a_low], axis=-1), 1, 2).astype(BF16)
    return main, wkat


def _perm_matrices():
    p = np.zeros((S5_ROWS, S5_ROWS), np.float32)
    t, b = np.meshgrid(np.arange(S5_T), np.arange(8), indexing='ij')
    p[(t * 8 + b).ravel(), (b * S5_T + t).ravel()] = 1.0
    return jnp.asarray(p, BF16), jnp.asarray(p.T, BF16)


def kernel(x, attn_norm_g, w_in, s5_a_re, s5_a_im, s5_log_dt, s5_b_re, s5_b_im, s5_c_re, s5_c_im,
           s5_d, s5_w_glu, gla_w_gate_up, gla_b_gate, gla_head_norm_g, w_branch_s5, w_branch_gla,
           w_out, ffn_norm_g, w_ffn_gate, w_ffn_up, w_ffn_down, final_norm_g):
    bsz, seq, _ = x.shape
    n = bsz * seq
    depth = w_in.shape[0]
    perm, permt = _perm_matrices()
    tok = np.arange(GLA_CUM_T)
    later = jnp.asarray((tok[:, None] > tok[None, :])
                        & (tok[:, None] // CHUNK == tok[None, :] // CHUNK), BF16)
    fg = final_norm_g.reshape(1, D_MODEL)

    vec = lambda t: t.reshape(depth, 1, -1)
    w_in_p, wkat = _pack_w_in(w_in)
    wupt = jnp.pad(jnp.swapaxes(gla_w_gate_up, 1, 2),
                   ((0, 0), (0, 0), (0, _ALOW_PAD - GLA_GATE_RANK))).astype(BF16)
    bmat, cmat, ar, ai = _s5_params(s5_a_re, s5_a_im, s5_log_dt, s5_b_re, s5_b_im,
                                    s5_c_re, s5_c_im)
    wglu = s5_w_glu.astype(BF16)
    wbs5, wbgla, wout = (w_branch_s5.astype(BF16), w_branch_gla.astype(BF16),
                         w_out.astype(BF16))
    wg, wu, wd = w_ffn_gate.astype(BF16), w_ffn_up.astype(BF16), w_ffn_down.astype(BF16)
    attn_g, ffn_g, hng, s5d = (vec(attn_norm_g), vec(ffn_norm_g), vec(gla_head_norm_g),
                               vec(s5_d))
    bgate = gla_b_gate.reshape(depth, GLA_KEY, 1)

    h = x.reshape(n, D_MODEL)
    for l in range(depth):
        u, q, kt, lat, v, sg, ss5, sgla = _inproj(h, l, attn_g, w_in_p, wkat, wupt, bgate)
        ys5 = _s5(u.reshape(bsz, seq, S5_WIDTH), l, perm, permt, bmat, cmat, ar, ai,
                  s5d, wglu).reshape(n, S5_WIDTH)
        ygla = _gla(q, kt, lat, v, sg, l, later, hng, bsz)
        h = _mix_ffn(h, ys5, ygla, ss5, sgla, l, wbs5, wbgla, wout, ffn_g, wg, wu, wd, fg,
                     final_norm=(l == depth - 1))
    return h.reshape(bsz, seq, D_MODEL)
```

```python
import functools
import math

import numpy as np
import jax
import jax.numpy as jnp
from jax import lax
from jax.experimental import pallas as pl
from jax.experimental.pallas import tpu as pltpu

F32 = jnp.float32
BF16 = jnp.bfloat16

D_MODEL = 1024
EPS = 1e-6
CHUNK = 64

S5_WIDTH = 512
S5_GROUP = 16
S5_GROUPS = 32
S5_STATE = 64
S5_HALF_GROUPS = S5_GROUPS // 2
S5_HALF_COLS = S5_HALF_GROUPS * S5_STATE

GLA_HEADS = 4
GLA_KEY = 512
GLA_VAL = 1024
GLA_DK = 128
GLA_DV = 256
GLA_GATE_RANK = 16
GLA_GATE_TEMP = 16.0

D_FF = 2816
FF_CHUNKS = ((0, 1024), (1024, 2048), (2048, 2816))

_OFF_U, _OFF_Q, _OFF_K, _OFF_V, _OFF_G, _OFF_GS5, _OFF_GGLA, _OFF_ALOW, _W_IN_COLS = (
    0, 512, 1024, 1536, 2560, 3584, 4608, 5632, 5760)
_ALOW_PAD = _W_IN_COLS - _OFF_ALOW

ROW_TILE = 512
S5_T = 64
S5_ROWS = 8 * S5_T
GLA_T = 512
GLA_NCH = GLA_T // CHUNK
GLA_CUM_T = 256

VMEM_LIMIT = 56 * 1024 * 1024


def _dot(a, b):
    return jnp.dot(a, b, preferred_element_type=F32)


def _split_bf16(x):
    hi = x.astype(BF16)
    lo = (x - hi.astype(F32)).astype(BF16)
    return hi, lo


def _inv_rms(x):
    return lax.rsqrt(jnp.mean(x * x, axis=-1, keepdims=True) + EPS)


def _const_spec(shape):
    zeros = (0,) * len(shape)
    return pl.BlockSpec(shape, lambda *_: zeros, pipeline_mode=pl.Buffered(1))


def _layer_spec(arr, layer):
    idx = (layer,) + (0,) * (arr.ndim - 1)
    return pl.BlockSpec((None,) + arr.shape[1:], lambda *_: idx, pipeline_mode=pl.Buffered(1))


def _params(*semantics):
    return pltpu.CompilerParams(dimension_semantics=semantics, vmem_limit_bytes=VMEM_LIMIT)


def _inproj_kernel(x_ref, g_ref, w_ref, wup_ref, bg_ref,
                   u_ref, q_ref, kt_ref, lat_ref, v_ref, sg_ref, ss5_ref, sgla_ref):
    x = x_ref[...]
    xg = (x * g_ref[...]).astype(BF16)
    inv = _inv_rms(x)

    def seg(a, b):
        return _dot(xg, w_ref[:, a:b]) * inv

    a_low = seg(_OFF_ALOW, _W_IN_COLS)
    z = _dot(a_low.astype(BF16), wup_ref[...]) + bg_ref[...]
    log_sig = jnp.minimum(z, 0.0) - jnp.log1p(jnp.exp(-jnp.abs(z)))
    lat_ref[...] = (log_sig * (1.0 / GLA_GATE_TEMP)).T
    kt_ref[...] = seg(_OFF_K, _OFF_V).T
    u_ref[...] = seg(_OFF_U, _OFF_Q)
    q_ref[...] = (seg(_OFF_Q, _OFF_K) * (GLA_DK ** -0.5)).astype(BF16)
    v_ref[...] = seg(_OFF_V, _OFF_G).astype(BF16)
    g = seg(_OFF_G, _OFF_GS5)
    sg_ref[...] = (g * jax.nn.sigmoid(g)).astype(BF16)
    ss5_ref[...] = jax.nn.sigmoid(seg(_OFF_GS5, _OFF_GGLA)).astype(BF16)
    sgla_ref[...] = jax.nn.sigmoid(seg(_OFF_GGLA, _OFF_ALOW)).astype(BF16)


def _inproj(h, layer, g, w, wup, bg):
    n = h.shape[0]
    row = lambda i: (i, 0)
    col = lambda i: (0, i)
    outs = [
        ((n, S5_WIDTH), F32, (ROW_TILE, S5_WIDTH), row),
        ((n, GLA_KEY), BF16, (ROW_TILE, GLA_KEY), row),
        ((GLA_KEY, n), F32, (GLA_KEY, ROW_TILE), col),
        ((GLA_KEY, n), F32, (GLA_KEY, ROW_TILE), col),
        ((n, GLA_VAL), BF16, (ROW_TILE, GLA_VAL), row),
        ((n, GLA_VAL), BF16, (ROW_TILE, GLA_VAL), row),
        ((n, D_MODEL), BF16, (ROW_TILE, D_MODEL), row),
        ((n, D_MODEL), BF16, (ROW_TILE, D_MODEL), row)]
    return pl.pallas_call(
        _inproj_kernel,
        grid=(n // ROW_TILE,),
        in_specs=[
            pl.BlockSpec((ROW_TILE, D_MODEL), row),
            _layer_spec(g, layer), _layer_spec(w, layer),
            _layer_spec(wup, layer), _layer_spec(bg, layer),
        ],
        out_specs=[pl.BlockSpec(blk, imap) for _, _, blk, imap in outs],
        out_shape=[jax.ShapeDtypeStruct(shape, dt) for shape, dt, _, _ in outs],
        compiler_params=_params("arbitrary"),
        name="inproj",
    )(h, g, w, wup, bg)


def _gelu_tanh(x):
    c = math.sqrt(2.0 / math.pi)
    return 0.5 * x * (1.0 + jnp.tanh(c * (x + 0.044715 * (x * x * x))))


def _s5_kernel(u_ref, perm_ref, permt_ref, bmat_ref, cmat_ref, ar_ref, ai_ref, d_ref, wglu_ref,
               o_ref, buf_ref, st_ref):
    @pl.when(pl.program_id(0) == 0)
    def _():
        st_ref[...] = jnp.zeros_like(st_ref)

    u_bf = u_ref[...].reshape(S5_ROWS, S5_WIDTH).astype(BF16)
    ut = _dot(perm_ref[...], u_bf)
    ut_bf = ut.astype(BF16)

    half_w = 2 * S5_HALF_COLS
    for kt in range(2):
        buf_ref[:, kt * half_w:(kt + 1) * half_w] = _dot(
            ut_bf[:, kt * 256:(kt + 1) * 256], bmat_ref[kt])

    for kt in range(2):
        c_re = kt * half_w
        c_im = c_re + S5_HALF_COLS
        re_cols = slice(c_re, c_re + S5_HALF_COLS)
        im_cols = slice(c_im, c_im + S5_HALF_COLS)
        ar = ar_ref[:, kt * S5_HALF_COLS:(kt + 1) * S5_HALF_COLS]
        ai = ai_ref[:, kt * S5_HALF_COLS:(kt + 1) * S5_HALF_COLS]
        s_re, s_im = st_ref[:, re_cols], st_ref[:, im_cols]
        for t in range(S5_T):
            rows = slice(t * 8, (t + 1) * 8)
            s_re, s_im = (ar * s_re - ai * s_im + buf_ref[rows, re_cols],
                          ar * s_im + ai * s_re + buf_ref[rows, im_cols])
            buf_ref[rows, re_cols] = s_re
            buf_ref[rows, im_cols] = s_im
        st_ref[:, re_cols] = s_re
        st_ref[:, im_cols] = s_im

    ys = [_dot(buf_ref[:, kt * half_w:(kt + 1) * half_w].astype(BF16), cmat_ref[kt])
          for kt in range(2)]
    y = jnp.concatenate(ys, axis=-1) + d_ref[...] * ut
    y = _gelu_tanh(y)
    z = _dot(y.astype(BF16), wglu_ref[...])
    out = (z[:, :S5_WIDTH] * jax.nn.sigmoid(z[:, S5_WIDTH:])).astype(BF16)
    o_ref[...] = _dot(permt_ref[...], out).astype(BF16).reshape(8, S5_T, S5_WIDTH)


def _s5(u, layer, perm, permt, bmat, cmat, ar, ai, d, wglu):
    bsz, seq, _ = u.shape
    blk = lambda j: (0, j, 0)
    return pl.pallas_call(
        _s5_kernel,
        grid=(seq // S5_T,),
        in_specs=[
            pl.BlockSpec((bsz, S5_T, S5_WIDTH), blk),
            _const_spec(perm.shape), _const_spec(permt.shape),
            _layer_spec(bmat, layer), _layer_spec(cmat, layer),
            _layer_spec(ar, layer), _layer_spec(ai, layer),
            _layer_spec(d, layer), _layer_spec(wglu, layer),
        ],
        out_specs=pl.BlockSpec((bsz, S5_T, S5_WIDTH), blk),
        out_shape=jax.ShapeDtypeStruct((bsz, seq, S5_WIDTH), BF16),
        scratch_shapes=[
            pltpu.VMEM((S5_ROWS, 4 * S5_HALF_COLS), F32),
            pltpu.VMEM((8, 4 * S5_HALF_COLS), F32),
        ],
        compiler_params=_params("arbitrary"),
        name="s5",
    )(u, perm, permt, bmat, cmat, ar, ai, d, wglu)


def _s5_params(a_re, a_im, log_dt, b_re, b_im, c_re, c_im):
    depth = a_re.shape[0]
    dt = jnp.exp(log_dt)[..., None]
    x = a_re * dt
    y = a_im * dt
    ex = jnp.exp(x)
    abar_re = ex * jnp.cos(y)
    abar_im = ex * jnp.sin(y)
    m1_re = jnp.expm1(x) * jnp.cos(y) - 2.0 * jnp.sin(0.5 * y) ** 2
    den = a_re * a_re + a_im * a_im
    coef_re = (m1_re * a_re + abar_im * a_im) / den
    coef_im = (abar_im * a_re - m1_re * a_im) / den
    bb_re = coef_re[..., None] * b_re - coef_im[..., None] * b_im
    bb_im = coef_re[..., None] * b_im + coef_im[..., None] * b_re

    eye = jnp.eye(S5_HALF_GROUPS, dtype=F32)

    def halves(t):
        return t.reshape((depth, 2, S5_HALF_GROUPS) + t.shape[2:])

    def pack_b(bb):
        return jnp.einsum('ab,lkbpc->lkacbp', eye, halves(bb)).reshape(
            depth, 2, 256, S5_HALF_COLS)

    def pack_c(cc):
        return jnp.einsum('ab,lkbcp->lkbpac', eye, halves(cc)).reshape(
            depth, 2, S5_HALF_COLS, 256)

    bmat = jnp.concatenate([pack_b(bb_re), pack_b(bb_im)], axis=3).astype(BF16)
    cmat = jnp.concatenate([pack_c(c_re), pack_c(-c_im)], axis=2).astype(BF16)
    cols = S5_GROUPS * S5_STATE
    ar = jnp.broadcast_to(abar_re.reshape(depth, 1, cols), (depth, 8, cols))
    ai = jnp.broadcast_to(abar_im.reshape(depth, 1, cols), (depth, 8, cols))
    return bmat, cmat, ar, ai


def _gla_kernel(q_ref, kt_ref, lat_ref, v_ref, sg_ref, later_ref, hng_ref, o_ref,
                s_ref, kv_ref, sb_ref):
    @pl.when(pl.program_id(1) == 0)
    def _():
        s_ref[...] = jnp.zeros_like(s_ref)

    later = later_ref[...]
    decays = []
    for grp in range(GLA_T // GLA_CUM_T):
        cols = slice(grp * GLA_CUM_T, (grp + 1) * GLA_CUM_T)
        la = lat_ref[:, cols]
        la_hi, la_lo = _split_bf16(la)
        rest = _dot(la_hi, later) + _dot(la_lo, later)
        k_end_t = (kt_ref[:, cols] * jnp.exp(rest)).astype(BF16)
        for cc in range(GLA_CUM_T // CHUNK):
            c = grp * (GLA_CUM_T // CHUNK) + cc
            first = slice(cc * CHUNK, cc * CHUNK + 1)
            rows = slice(c * CHUNK, (c + 1) * CHUNK)
            decays.append(jnp.exp(rest[:, first] + la[:, first]))
            for h in range(GLA_HEADS):
                kv_ref[c, h] = _dot(
                    k_end_t[h * GLA_DK:(h + 1) * GLA_DK, cc * CHUNK:(cc + 1) * CHUNK],
                    v_ref[rows, h * GLA_DV:(h + 1) * GLA_DV])

    states = [s_ref[h] for h in range(GLA_HEADS)]
    for c in range(GLA_NCH):
        for h in range(GLA_HEADS):
            states[h] = states[h] * decays[c][h * GLA_DK:(h + 1) * GLA_DK, :] + kv_ref[c, h]
            sb_ref[c, h] = states[h].astype(BF16)
    for h in range(GLA_HEADS):
        s_ref[h] = states[h]

    for c in range(GLA_NCH):
        rows = slice(c * CHUNK, (c + 1) * CHUNK)
        for h in range(GLA_HEADS):
            vs = slice(h * GLA_DV, (h + 1) * GLA_DV)
            o = _dot(q_ref[rows, h * GLA_DK:(h + 1) * GLA_DK], sb_ref[c, h])
            o = o * _inv_rms(o) * hng_ref[:, vs] * sg_ref[rows, vs].astype(F32)
            o_ref[rows, vs] = o.astype(BF16)


def _gla(q, kt, lat, v, sg, layer, later, hng, bsz):
    n = q.shape[0]
    nj = n // bsz // GLA_T
    row = lambda b, j: (b * nj + j, 0)
    col = lambda b, j: (0, b * nj + j)
    return pl.pallas_call(
        _gla_kernel,
        grid=(bsz, nj),
        in_specs=[
            pl.BlockSpec((GLA_T, GLA_KEY), row),
            pl.BlockSpec((GLA_KEY, GLA_T), col),
            pl.BlockSpec((GLA_KEY, GLA_T), col),
            pl.BlockSpec((GLA_T, GLA_VAL), row),
            pl.BlockSpec((GLA_T, GLA_VAL), row),
            _const_spec(later.shape), _layer_spec(hng, layer),
        ],
        out_specs=pl.BlockSpec((GLA_T, GLA_VAL), row),
        out_shape=jax.ShapeDtypeStruct((n, GLA_VAL), BF16),
        scratch_shapes=[
            pltpu.VMEM((GLA_HEADS, GLA_DK, GLA_DV), F32),
            pltpu.VMEM((GLA_NCH, GLA_HEADS, GLA_DK, GLA_DV), F32),
            pltpu.VMEM((GLA_NCH, GLA_HEADS, GLA_DK, GLA_DV), BF16),
        ],
        compiler_params=_params("arbitrary", "arbitrary"),
        name="gla",
    )(q, kt, lat, v, sg, later, hng)


def _mix_ffn_kernel(h_ref, ys5_ref, ygla_ref, ss5_ref, sgla_ref, wbs5_ref, wbgla_ref, wout_ref,
                    g_ref, wg_ref, wu_ref, wd_ref, fg_ref, o_ref, *, final_norm):
    mixed = (ss5_ref[...].astype(F32) * _dot(ys5_ref[...], wbs5_ref[...])
             + sgla_ref[...].astype(F32) * _dot(ygla_ref[...], wbgla_ref[...]))
    h = h_ref[...] + _dot(mixed.astype(BF16), wout_ref[...])
    hg = (h * g_ref[...]).astype(BF16)
    inv = _inv_rms(h)
    acc = h
    for a, b in FF_CHUNKS:
        gate = _dot(hg, wg_ref[:, a:b]) * inv
        up = _dot(hg, wu_ref[:, a:b]) * inv
        act = (gate * jax.nn.sigmoid(gate) * up).astype(BF16)
        acc = acc + _dot(act, wd_ref[a:b, :])
    if final_norm:
        acc = acc * _inv_rms(acc) * fg_ref[...]
    o_ref[...] = acc


def _mix_ffn(h, ys5, ygla, ss5, sgla, layer, wbs5, wbgla, wout, g, wg, wu, wd, fg, final_norm):
    n = h.shape[0]
    row = lambda i: (i, 0)
    return pl.pallas_call(
        functools.partial(_mix_ffn_kernel, final_norm=final_norm),
        grid=(n // ROW_TILE,),
        in_specs=[
            pl.BlockSpec((ROW_TILE, D_MODEL), row),
            pl.BlockSpec((ROW_TILE, S5_WIDTH), row),
            pl.BlockSpec((ROW_TILE, GLA_VAL), row),
            pl.BlockSpec((ROW_TILE, D_MODEL), row),
            pl.BlockSpec((ROW_TILE, D_MODEL), row),
            _layer_spec(wbs5, layer), _layer_spec(wbgla, layer), _layer_spec(wout, layer),
            _layer_spec(g, layer), _layer_spec(wg, layer), _layer_spec(wu, layer),
            _layer_spec(wd, layer), _const_spec(fg.shape),
        ],
        out_specs=pl.BlockSpec((ROW_TILE, D_MODEL), row),
        out_shape=jax.ShapeDtypeStruct((n, D_MODEL), F32),
        compiler_params=_params("arbitrary"),
        name="mix_ffn_final" if final_norm else "mix_ffn",
    )(h, ys5, ygla, ss5, sgla, wbs5, wbgla, wout, g, wg, wu, wd, fg)


def _pack_w_in(w):
    u, q, k, v, g, a_low, gs5, ggla = jnp.split(
        w, [512, 1024, 1536, 2560, 3584, 3600, 4624], axis=-1)
    pad = jnp.zeros(w.shape[:-1] + (_ALOW_PAD - GLA_GATE_RANK,), w.dtype)
    return jnp.concatenate([u, q, k, v, g, gs5, ggla, a_low, pad], axis=-1).astype(BF16)


def _perm_matrices():
    p = np.zeros((S5_ROWS, S5_ROWS), np.float32)
    t, b = np.meshgrid(np.arange(S5_T), np.arange(8), indexing='ij')
    p[(t * 8 + b).ravel(), (b * S5_T + t).ravel()] = 1.0
    return jnp.asarray(p, BF16), jnp.asarray(p.T, BF16)


def kernel(x, attn_norm_g, w_in, s5_a_re, s5_a_im, s5_log_dt, s5_b_re, s5_b_im, s5_c_re, s5_c_im,
           s5_d, s5_w_glu, gla_w_gate_up, gla_b_gate, gla_head_norm_g, w_branch_s5, w_branch_gla,
           w_out, ffn_norm_g, w_ffn_gate, w_ffn_up, w_ffn_down, final_norm_g):
    bsz, seq, _ = x.shape
    n = bsz * seq
    depth = w_in.shape[0]
    perm, permt = _perm_matrices()
    tok = np.arange(GLA_CUM_T)
    later = jnp.asarray((tok[:, None] > tok[None, :])
                        & (tok[:, None] // CHUNK == tok[None, :] // CHUNK), BF16)
    fg = final_norm_g.reshape(1, D_MODEL)

    vec = lambda t: t.reshape(depth, 1, -1)
    w_in_p = _pack_w_in(w_in)
    wup = jnp.pad(gla_w_gate_up,
                  ((0, 0), (0, _ALOW_PAD - GLA_GATE_RANK), (0, 0))).astype(BF16)
    bmat, cmat, ar, ai = _s5_params(s5_a_re, s5_a_im, s5_log_dt, s5_b_re, s5_b_im,
                                    s5_c_re, s5_c_im)
    wglu = s5_w_glu.astype(BF16)
    wbs5, wbgla, wout = (w_branch_s5.astype(BF16), w_branch_gla.astype(BF16),
                         w_out.astype(BF16))
    wg, wu, wd = w_ffn_gate.astype(BF16), w_ffn_up.astype(BF16), w_ffn_down.astype(BF16)
    attn_g, ffn_g, bgate, hng, s5d = (vec(attn_norm_g), vec(ffn_norm_g), vec(gla_b_gate),
                                      vec(gla_head_norm_g), vec(s5_d))

    h = x.reshape(n, D_MODEL)
    for l in range(depth):
        u, q, kt, lat, v, sg, ss5, sgla = _inproj(h, l, attn_g, w_in_p, wup, bgate)
        ys5 = _s5(u.reshape(bsz, seq, S5_WIDTH), l, perm, permt, bmat, cmat, ar, ai,
                  s5d, wglu).reshape(n, S5_WIDTH)
        ygla = _gla(q, kt, lat, v, sg, l, later, hng, bsz)
        h = _mix_ffn(h, ys5, ygla, ss5, sgla, l, wbs5, wbgla, wout, ffn_g, wg, wu, wd, fg,
                     final_norm=(l == depth - 1))
    return h.reshape(bsz, seq, D_MODEL)
```

```python
import functools
import math

import numpy as np
import jax
import jax.numpy as jnp
from jax import lax
from jax.experimental import pallas as pl
from jax.experimental.pallas import tpu as pltpu

F32 = jnp.float32
BF16 = jnp.bfloat16

D_MODEL = 1024
EPS = 1e-6
CHUNK = 64

S5_WIDTH = 512
S5_GROUP = 16
S5_GROUPS = 32
S5_STATE = 64
S5_HALF_GROUPS = S5_GROUPS // 2
S5_HALF_COLS = S5_HALF_GROUPS * S5_STATE

GLA_HEADS = 4
GLA_KEY = 512
GLA_VAL = 1024
GLA_DK = 128
GLA_DV = 256
GLA_GATE_RANK = 16
GLA_GATE_TEMP = 16.0

D_FF = 2816
FF_CHUNKS = ((0, 1024), (1024, 2048), (2048, 2816))

_OFF_U, _OFF_Q, _OFF_K, _OFF_V, _OFF_G, _W_MAIN_COLS = 0, 512, 1024, 1536, 2560, 3584
_OFF_GS5, _OFF_GGLA, _OFF_ALOW, _W_TAIL_COLS = 0, 1024, 2048, 2176
_ALOW_PAD = _W_TAIL_COLS - _OFF_ALOW

ROW_TILE = 512
S5_SUB_T = 64
S5_SUB_ROWS = 8 * S5_SUB_T
S5_SUBS = 2
S5_T = S5_SUBS * S5_SUB_T
S5_ROWS = 8 * S5_T
GLA_T = 1024
GLA_NCH = GLA_T // CHUNK
GLA_CUM_T = 256

VMEM_LIMIT = 56 * 1024 * 1024


def _dot(a, b):
    return jnp.dot(a, b, preferred_element_type=F32)


def _split_bf16(x):
    hi = x.astype(BF16)
    lo = (x - hi.astype(F32)).astype(BF16)
    return hi, lo


def _inv_rms(x):
    return lax.rsqrt(jnp.mean(x * x, axis=-1, keepdims=True) + EPS)


def _const_spec(shape):
    zeros = (0,) * len(shape)
    return pl.BlockSpec(shape, lambda *_: zeros, pipeline_mode=pl.Buffered(1))


def _layer_spec(arr, layer):
    idx = (layer,) + (0,) * (arr.ndim - 1)
    return pl.BlockSpec((None,) + arr.shape[1:], lambda *_: idx, pipeline_mode=pl.Buffered(1))


def _params(*semantics):
    return pltpu.CompilerParams(dimension_semantics=semantics, vmem_limit_bytes=VMEM_LIMIT)


def _inproj_kernel(x_ref, g_ref, w_ref, wt_ref, wup_ref, bg_ref,
                   u_ref, q_ref, kt_ref, lat_ref, v_ref, sg_ref, ss5_ref, sgla_ref):
    x = x_ref[...]
    xg = (x * g_ref[...]).astype(BF16)
    inv = _inv_rms(x)

    def seg(ref, a, b):
        return _dot(xg, ref[:, a:b]) * inv

    a_low = seg(wt_ref, _OFF_ALOW, _W_TAIL_COLS)
    z = _dot(a_low.astype(BF16), wup_ref[...]) + bg_ref[...]
    log_sig = jnp.minimum(z, 0.0) - jnp.log1p(jnp.exp(-jnp.abs(z)))
    lat_ref[...] = (log_sig * (1.0 / GLA_GATE_TEMP)).T
    kt_ref[...] = seg(w_ref, _OFF_K, _OFF_V).T
    u_ref[...] = seg(w_ref, _OFF_U, _OFF_Q).astype(BF16)
    q_ref[...] = (seg(w_ref, _OFF_Q, _OFF_K) * (GLA_DK ** -0.5)).astype(BF16)
    v_ref[...] = seg(w_ref, _OFF_V, _OFF_G).astype(BF16)
    g = seg(w_ref, _OFF_G, _W_MAIN_COLS)
    sg_ref[...] = (g * jax.nn.sigmoid(g)).astype(BF16)
    ss5_ref[...] = jax.nn.sigmoid(seg(wt_ref, _OFF_GS5, _OFF_GGLA)).astype(BF16)
    sgla_ref[...] = jax.nn.sigmoid(seg(wt_ref, _OFF_GGLA, _OFF_ALOW)).astype(BF16)


def _inproj(h, layer, g, w, wt, wup, bg):
    n = h.shape[0]
    row = lambda i: (i, 0)
    col = lambda i: (0, i)
    outs = [
        ((n, S5_WIDTH), BF16, (ROW_TILE, S5_WIDTH), row),
        ((n, GLA_KEY), BF16, (ROW_TILE, GLA_KEY), row),
        ((GLA_KEY, n), F32, (GLA_KEY, ROW_TILE), col),
        ((GLA_KEY, n), F32, (GLA_KEY, ROW_TILE), col),
        ((n, GLA_VAL), BF16, (ROW_TILE, GLA_VAL), row),
        ((n, GLA_VAL), BF16, (ROW_TILE, GLA_VAL), row),
        ((n, D_MODEL), BF16, (ROW_TILE, D_MODEL), row),
        ((n, D_MODEL), BF16, (ROW_TILE, D_MODEL), row)]
    return pl.pallas_call(
        _inproj_kernel,
        grid=(n // ROW_TILE,),
        in_specs=[
            pl.BlockSpec((ROW_TILE, D_MODEL), row),
            _layer_spec(g, layer), _layer_spec(w, layer), _layer_spec(wt, layer),
            _layer_spec(wup, layer), _layer_spec(bg, layer),
        ],
        out_specs=[pl.BlockSpec(blk, imap) for _, _, blk, imap in outs],
        out_shape=[jax.ShapeDtypeStruct(shape, dt) for shape, dt, _, _ in outs],
        compiler_params=_params("arbitrary"),
        name="inproj",
    )(h, g, w, wt, wup, bg)


def _gelu_tanh(x):
    c = math.sqrt(2.0 / math.pi)
    return 0.5 * x * (1.0 + jnp.tanh(c * (x + 0.044715 * (x * x * x))))


def _s5_kernel(u_ref, perm_ref, permt_ref, bmat_ref, cmat_ref, ar_ref, ai_ref, d_ref, wglu_ref,
               o_ref, buf_ref, st_ref):
    @pl.when(pl.program_id(0) == 0)
    def _():
        st_ref[...] = jnp.zeros_like(st_ref)

    ut = jnp.concatenate(
        [_dot(perm_ref[...], u_ref[:, s * S5_SUB_T:(s + 1) * S5_SUB_T, :].reshape(
            S5_SUB_ROWS, S5_WIDTH)) for s in range(S5_SUBS)], axis=0)
    ut_bf = ut.astype(BF16)

    half_w = 2 * S5_HALF_COLS
    for kt in range(2):
        buf_ref[:, kt * half_w:(kt + 1) * half_w] = _dot(
            ut_bf[:, kt * 256:(kt + 1) * 256], bmat_ref[kt])

    for kt in range(2):
        c_re = kt * half_w
        c_im = c_re + S5_HALF_COLS
        re_cols = slice(c_re, c_re + S5_HALF_COLS)
        im_cols = slice(c_im, c_im + S5_HALF_COLS)
        ar = ar_ref[:, kt * S5_HALF_COLS:(kt + 1) * S5_HALF_COLS]
        ai = ai_ref[:, kt * S5_HALF_COLS:(kt + 1) * S5_HALF_COLS]
        s_re, s_im = st_ref[:, re_cols], st_ref[:, im_cols]
        for t in range(S5_T):
            rows = slice(t * 8, (t + 1) * 8)
            s_re, s_im = (ar * s_re - ai * s_im + buf_ref[rows, re_cols],
                          ar * s_im + ai * s_re + buf_ref[rows, im_cols])
            buf_ref[rows, re_cols] = s_re
            buf_ref[rows, im_cols] = s_im
        st_ref[:, re_cols] = s_re
        st_ref[:, im_cols] = s_im

    ys = [_dot(buf_ref[:, kt * half_w:(kt + 1) * half_w].astype(BF16), cmat_ref[kt])
          for kt in range(2)]
    y = jnp.concatenate(ys, axis=-1) + d_ref[...] * ut
    y = _gelu_tanh(y)
    z = _dot(y.astype(BF16), wglu_ref[...])
    out = (z[:, :S5_WIDTH] * jax.nn.sigmoid(z[:, S5_WIDTH:])).astype(BF16)
    for s in range(S5_SUBS):
        o_ref[:, s * S5_SUB_T:(s + 1) * S5_SUB_T, :] = _dot(
            permt_ref[...], out[s * S5_SUB_ROWS:(s + 1) * S5_SUB_ROWS, :]
        ).astype(BF16).reshape(8, S5_SUB_T, S5_WIDTH)


def _s5(u, layer, perm, permt, bmat, cmat, ar, ai, d, wglu):
    bsz, seq, _ = u.shape
    blk = lambda j: (0, j, 0)
    return pl.pallas_call(
        _s5_kernel,
        grid=(seq // S5_T,),
        in_specs=[
            pl.BlockSpec((bsz, S5_T, S5_WIDTH), blk),
            _const_spec(perm.shape), _const_spec(permt.shape),
            _layer_spec(bmat, layer), _layer_spec(cmat, layer),
            _layer_spec(ar, layer), _layer_spec(ai, layer),
            _layer_spec(d, layer), _layer_spec(wglu, layer),
        ],
        out_specs=pl.BlockSpec((bsz, S5_T, S5_WIDTH), blk),
        out_shape=jax.ShapeDtypeStruct((bsz, seq, S5_WIDTH), BF16),
        scratch_shapes=[
            pltpu.VMEM((S5_ROWS, 4 * S5_HALF_COLS), F32),
            pltpu.VMEM((8, 4 * S5_HALF_COLS), F32),
        ],
        compiler_params=_params("arbitrary"),
        name="s5",
    )(u, perm, permt, bmat, cmat, ar, ai, d, wglu)


def _s5_params(a_re, a_im, log_dt, b_re, b_im, c_re, c_im):
    depth = a_re.shape[0]
    dt = jnp.exp(log_dt)[..., None]
    x = a_re * dt
    y = a_im * dt
    ex = jnp.exp(x)
    abar_re = ex * jnp.cos(y)
    abar_im = ex * jnp.sin(y)
    m1_re = jnp.expm1(x) * jnp.cos(y) - 2.0 * jnp.sin(0.5 * y) ** 2
    den = a_re * a_re + a_im * a_im
    coef_re = (m1_re * a_re + abar_im * a_im) / den
    coef_im = (abar_im * a_re - m1_re * a_im) / den
    bb_re = coef_re[..., None] * b_re - coef_im[..., None] * b_im
    bb_im = coef_re[..., None] * b_im + coef_im[..., None] * b_re

    eye = jnp.eye(S5_HALF_GROUPS, dtype=F32)

    def halves(t):
        return t.reshape((depth, 2, S5_HALF_GROUPS) + t.shape[2:])

    def pack_b(bb):
        return jnp.einsum('ab,lkbpc->lkacbp', eye, halves(bb)).reshape(
            depth, 2, 256, S5_HALF_COLS)

    def pack_c(cc):
        return jnp.einsum('ab,lkbcp->lkbpac', eye, halves(cc)).reshape(
            depth, 2, S5_HALF_COLS, 256)

    bmat = jnp.concatenate([pack_b(bb_re), pack_b(bb_im)], axis=3).astype(BF16)
    cmat = jnp.concatenate([pack_c(c_re), pack_c(-c_im)], axis=2).astype(BF16)
    cols = S5_GROUPS * S5_STATE
    ar = jnp.broadcast_to(abar_re.reshape(depth, 1, cols), (depth, 8, cols))
    ai = jnp.broadcast_to(abar_im.reshape(depth, 1, cols), (depth, 8, cols))
    return bmat, cmat, ar, ai


def _gla_kernel(q_ref, kt_ref, lat_ref, v_ref, sg_ref, later_ref, hng_ref, o_ref,
                s_ref, kv_ref, sb_ref):
    @pl.when(pl.program_id(1) == 0)
    def _():
        s_ref[...] = jnp.zeros_like(s_ref)

    later = later_ref[...]
    decays = []
    for grp in range(GLA_T // GLA_CUM_T):
        cols = slice(grp * GLA_CUM_T, (grp + 1) * GLA_CUM_T)
        la = lat_ref[:, cols]
        la_hi, la_lo = _split_bf16(la)
        rest = _dot(la_hi, later) + _dot(la_lo, later)
        k_end_t = (kt_ref[:, cols] * jnp.exp(rest)).astype(BF16)
        for cc in range(GLA_CUM_T // CHUNK):
            c = grp * (GLA_CUM_T // CHUNK) + cc
            first = slice(cc * CHUNK, cc * CHUNK + 1)
            rows = slice(c * CHUNK, (c + 1) * CHUNK)
            decays.append(jnp.exp(rest[:, first] + la[:, first]))
            for h in range(GLA_HEADS):
                kv_ref[c, h] = _dot(
                    k_end_t[h * GLA_DK:(h + 1) * GLA_DK, cc * CHUNK:(cc + 1) * CHUNK],
                    v_ref[rows, h * GLA_DV:(h + 1) * GLA_DV])

    states = [s_ref[h] for h in range(GLA_HEADS)]
    for c in range(GLA_NCH):
        for h in range(GLA_HEADS):
            states[h] = states[h] * decays[c][h * GLA_DK:(h + 1) * GLA_DK, :] + kv_ref[c, h]
            sb_ref[c, h] = states[h].astype(BF16)
    for h in range(GLA_HEADS):
        s_ref[h] = states[h]

    for c in range(GLA_NCH):
        rows = slice(c * CHUNK, (c + 1) * CHUNK)
        for h in range(GLA_HEADS):
            vs = slice(h * GLA_DV, (h + 1) * GLA_DV)
            o = _dot(q_ref[rows, h * GLA_DK:(h + 1) * GLA_DK], sb_ref[c, h])
            o = o * _inv_rms(o) * hng_ref[:, vs] * sg_ref[rows, vs].astype(F32)
            o_ref[rows, vs] = o.astype(BF16)


def _gla(q, kt, lat, v, sg, layer, later, hng, bsz):
    n = q.shape[0]
    nj = n // bsz // GLA_T
    row = lambda b, j: (b * nj + j, 0)
    col = lambda b, j: (0, b * nj + j)
    return pl.pallas_call(
        _gla_kernel,
        grid=(bsz, nj),
        in_specs=[
            pl.BlockSpec((GLA_T, GLA_KEY), row),
            pl.BlockSpec((GLA_KEY, GLA_T), col),
            pl.BlockSpec((GLA_KEY, GLA_T), col),
            pl.BlockSpec((GLA_T, GLA_VAL), row),
            pl.BlockSpec((GLA_T, GLA_VAL), row),
            _const_spec(later.shape), _layer_spec(hng, layer),
        ],
        out_specs=pl.BlockSpec((GLA_T, GLA_VAL), row),
        out_shape=jax.ShapeDtypeStruct((n, GLA_VAL), BF16),
        scratch_shapes=[
            pltpu.VMEM((GLA_HEADS, GLA_DK, GLA_DV), F32),
            pltpu.VMEM((GLA_NCH, GLA_HEADS, GLA_DK, GLA_DV), F32),
            pltpu.VMEM((GLA_NCH, GLA_HEADS, GLA_DK, GLA_DV), BF16),
        ],
        compiler_params=_params("arbitrary", "arbitrary"),
        name="gla",
    )(q, kt, lat, v, sg, later, hng)


def _mix_ffn_kernel(h_ref, ys5_ref, ygla_ref, ss5_ref, sgla_ref, wbs5_ref, wbgla_ref, wout_ref,
                    g_ref, wg_ref, wu_ref, wd_ref, fg_ref, o_ref, *, final_norm):
    mixed = (ss5_ref[...].astype(F32) * _dot(ys5_ref[...], wbs5_ref[...])
             + sgla_ref[...].astype(F32) * _dot(ygla_ref[...], wbgla_ref[...]))
    h = h_ref[...] + _dot(mixed.astype(BF16), wout_ref[...])
    hg = (h * g_ref[...]).astype(BF16)
    inv = _inv_rms(h)
    acc = h
    for a, b in FF_CHUNKS:
        gate = _dot(hg, wg_ref[:, a:b]) * inv
        up = _dot(hg, wu_ref[:, a:b]) * inv
        act = (gate * jax.nn.sigmoid(gate) * up).astype(BF16)
        acc = acc + _dot(act, wd_ref[a:b, :])
    if final_norm:
        acc = acc * _inv_rms(acc) * fg_ref[...]
    o_ref[...] = acc


def _mix_ffn(h, ys5, ygla, ss5, sgla, layer, wbs5, wbgla, wout, g, wg, wu, wd, fg, final_norm):
    n = h.shape[0]
    row = lambda i: (i, 0)
    return pl.pallas_call(
        functools.partial(_mix_ffn_kernel, final_norm=final_norm),
        grid=(n // ROW_TILE,),
        in_specs=[
            pl.BlockSpec((ROW_TILE, D_MODEL), row),
            pl.BlockSpec((ROW_TILE, S5_WIDTH), row),
            pl.BlockSpec((ROW_TILE, GLA_VAL), row),
            pl.BlockSpec((ROW_TILE, D_MODEL), row),
            pl.BlockSpec((ROW_TILE, D_MODEL), row),
            _layer_spec(wbs5, layer), _layer_spec(wbgla, layer), _layer_spec(wout, layer),
            _layer_spec(g, layer), _layer_spec(wg, layer), _layer_spec(wu, layer),
            _layer_spec(wd, layer), _const_spec(fg.shape),
        ],
        out_specs=pl.BlockSpec((ROW_TILE, D_MODEL), row),
        out_shape=jax.ShapeDtypeStruct((n, D_MODEL), F32),
        compiler_params=_params("arbitrary"),
        name="mix_ffn_final" if final_norm else "mix_ffn",
    )(h, ys5, ygla, ss5, sgla, wbs5, wbgla, wout, g, wg, wu, wd, fg)


def _pack_w_in(w):
    main = w[..., :_W_MAIN_COLS].astype(BF16)
    a_low = w[..., _W_MAIN_COLS:_W_MAIN_COLS + GLA_GATE_RANK]
    gates = w[..., _W_MAIN_COLS + GLA_GATE_RANK:]
    pad = jnp.zeros(w.shape[:-1] + (_ALOW_PAD - GLA_GATE_RANK,), w.dtype)
    tail = jnp.concatenate([gates, a_low, pad], axis=-1).astype(BF16)
    return main, tail


def _perm_matrices():
    p = np.zeros((S5_SUB_ROWS, S5_SUB_ROWS), np.float32)
    t, b = np.meshgrid(np.arange(S5_SUB_T), np.arange(8), indexing='ij')
    p[(t * 8 + b).ravel(), (b * S5_SUB_T + t).ravel()] = 1.0
    return jnp.asarray(p, BF16), jnp.asarray(p.T, BF16)


def kernel(x, attn_norm_g, w_in, s5_a_re, s5_a_im, s5_log_dt, s5_b_re, s5_b_im, s5_c_re, s5_c_im,
           s5_d, s5_w_glu, gla_w_gate_up, gla_b_gate, gla_head_norm_g, w_branch_s5, w_branch_gla,
           w_out, ffn_norm_g, w_ffn_gate, w_ffn_up, w_ffn_down, final_norm_g):
    bsz, seq, _ = x.shape
    n = bsz * seq
    depth = w_in.shape[0]
    perm, permt = _perm_matrices()
    tok = np.arange(GLA_CUM_T)
    later = jnp.asarray((tok[:, None] > tok[None, :])
                        & (tok[:, None] // CHUNK == tok[None, :] // CHUNK), BF16)
    fg = final_norm_g.reshape(1, D_MODEL)

    vec = lambda t: t.reshape(depth, 1, -1)
    w_main, w_tail = _pack_w_in(w_in)
    wup = jnp.pad(gla_w_gate_up,
                  ((0, 0), (0, _ALOW_PAD - GLA_GATE_RANK), (0, 0))).astype(BF16)
    bmat, cmat, ar, ai = _s5_params(s5_a_re, s5_a_im, s5_log_dt, s5_b_re, s5_b_im,
                                    s5_c_re, s5_c_im)
    wglu = s5_w_glu.astype(BF16)
    wbs5, wbgla, wout = (w_branch_s5.astype(BF16), w_branch_gla.astype(BF16),
                         w_out.astype(BF16))
    wg, wu, wd = w_ffn_gate.astype(BF16), w_ffn_up.astype(BF16), w_ffn_down.astype(BF16)
    attn_g, ffn_g, bgate, hng, s5d = (vec(attn_norm_g), vec(ffn_norm_g), vec(gla_b_gate),
                                      vec(gla_head_norm_g), vec(s5_d))

    h = x.reshape(n, D_MODEL)
    for l in range(depth):
        u, q, kt, lat, v, sg, ss5, sgla = _inproj(h, l, attn_g, w_main, w_tail, wup, bgate)
        ys5 = _s5(u.reshape(bsz, seq, S5_WIDTH), l, perm, permt, bmat, cmat, ar, ai,
                  s5d, wglu).reshape(n, S5_WIDTH)
        ygla = _gla(q, kt, lat, v, sg, l, later, hng, bsz)
        h = _mix_ffn(h, ys5, ygla, ss5, sgla, l, wbs5, wbgla, wout, ffn_g, wg, wu, wd, fg,
                     final_norm=(l == depth - 1))
    return h.reshape(bsz, seq, D_MODEL)
```

```python
import functools
import math

import numpy as np
import jax
import jax.numpy as jnp
from jax import lax
from jax.experimental import pallas as pl
from jax.experimental.pallas import tpu as pltpu

F32 = jnp.float32
BF16 = jnp.bfloat16

D_MODEL = 1024
EPS = 1e-6
CHUNK = 64

S5_WIDTH = 512
S5_GROUP = 16
S5_GROUPS = 32
S5_STATE = 64
S5_HALF_GROUPS = S5_GROUPS // 2
S5_HALF_COLS = S5_HALF_GROUPS * S5_STATE

GLA_HEADS = 4
GLA_KEY = 512
GLA_VAL = 1024
GLA_DK = 128
GLA_DV = 256
GLA_GATE_RANK = 16
GLA_GATE_TEMP = 16.0

D_FF = 2816
FF_CHUNKS = ((0, 1024), (1024, 2048), (2048, 2816))

_OFF_U, _OFF_Q, _OFF_K, _OFF_V, _OFF_G, _W_MAIN_COLS = 0, 512, 1024, 1536, 2560, 3584
_OFF_GS5, _OFF_GGLA, _OFF_ALOW, _W_TAIL_COLS = 0, 1024, 2048, 2176
_ALOW_PAD = _W_TAIL_COLS - _OFF_ALOW

ROW_TILE = 512
S5_SUB_T = 64
S5_SUB_ROWS = 8 * S5_SUB_T
S5_SUBS = 2
S5_T = S5_SUBS * S5_SUB_T
S5_ROWS = 8 * S5_T
GLA_T = 1024
GLA_NCH = GLA_T // CHUNK
GLA_CUM_T = 256

VMEM_LIMIT = 56 * 1024 * 1024


def _dot(a, b):
    return jnp.dot(a, b, preferred_element_type=F32)


def _split_bf16(x):
    hi = x.astype(BF16)
    lo = (x - hi.astype(F32)).astype(BF16)
    return hi, lo


def _inv_rms(x):
    return lax.rsqrt(jnp.mean(x * x, axis=-1, keepdims=True) + EPS)


def _const_spec(shape):
    zeros = (0,) * len(shape)
    return pl.BlockSpec(shape, lambda *_: zeros, pipeline_mode=pl.Buffered(1))


def _layer_spec(arr, layer):
    idx = (layer,) + (0,) * (arr.ndim - 1)
    return pl.BlockSpec((None,) + arr.shape[1:], lambda *_: idx, pipeline_mode=pl.Buffered(1))


def _params(*semantics):
    return pltpu.CompilerParams(dimension_semantics=semantics, vmem_limit_bytes=VMEM_LIMIT)


def _inproj_kernel(x_ref, g_ref, w_ref, wt_ref, wup_ref, bg_ref,
                   u_ref, q_ref, kt_ref, lat_ref, v_ref, sg_ref, ss5_ref, sgla_ref):
    x = x_ref[...]
    xg = (x * g_ref[...]).astype(BF16)
    inv = _inv_rms(x)

    def seg(ref, a, b):
        return _dot(xg, ref[:, a:b]) * inv

    a_low = seg(wt_ref, _OFF_ALOW, _W_TAIL_COLS)
    z = _dot(a_low.astype(BF16), wup_ref[...]) + bg_ref[...]
    log_sig = jnp.minimum(z, 0.0) - jnp.log1p(jnp.exp(-jnp.abs(z)))
    lat_ref[...] = (log_sig * (1.0 / GLA_GATE_TEMP)).T
    kt_ref[...] = seg(w_ref, _OFF_K, _OFF_V).T
    u_ref[...] = seg(w_ref, _OFF_U, _OFF_Q).astype(BF16)
    q_ref[...] = (seg(w_ref, _OFF_Q, _OFF_K) * (GLA_DK ** -0.5)).astype(BF16)
    v_ref[...] = seg(w_ref, _OFF_V, _OFF_G).astype(BF16)
    g = seg(w_ref, _OFF_G, _W_MAIN_COLS)
    sg_ref[...] = (g * jax.nn.sigmoid(g)).astype(BF16)
    ss5_ref[...] = jax.nn.sigmoid(seg(wt_ref, _OFF_GS5, _OFF_GGLA)).astype(BF16)
    sgla_ref[...] = jax.nn.sigmoid(seg(wt_ref, _OFF_GGLA, _OFF_ALOW)).astype(BF16)


def _inproj(h, layer, g, w, wt, wup, bg):
    n = h.shape[0]
    row = lambda i: (i, 0)
    col = lambda i: (0, i)
    outs = [
        ((n, S5_WIDTH), BF16, (ROW_TILE, S5_WIDTH), row),
        ((n, GLA_KEY), BF16, (ROW_TILE, GLA_KEY), row),
        ((GLA_KEY, n), F32, (GLA_KEY, ROW_TILE), col),
        ((GLA_KEY, n), F32, (GLA_KEY, ROW_TILE), col),
        ((n, GLA_VAL), BF16, (ROW_TILE, GLA_VAL), row),
        ((n, GLA_VAL), BF16, (ROW_TILE, GLA_VAL), row),
        ((n, D_MODEL), BF16, (ROW_TILE, D_MODEL), row),
        ((n, D_MODEL), BF16, (ROW_TILE, D_MODEL), row)]
    return pl.pallas_call(
        _inproj_kernel,
        grid=(n // ROW_TILE,),
        in_specs=[
            pl.BlockSpec((ROW_TILE, D_MODEL), row),
            _layer_spec(g, layer),
            pl.BlockSpec((None, D_MODEL, _W_MAIN_COLS), lambda i: (layer, 0, 0),
                         pipeline_mode=pl.Buffered(1)),
            _layer_spec(wt, layer), _layer_spec(wup, layer), _layer_spec(bg, layer),
        ],
        out_specs=[pl.BlockSpec(blk, imap) for _, _, blk, imap in outs],
        out_shape=[jax.ShapeDtypeStruct(shape, dt) for shape, dt, _, _ in outs],
        compiler_params=_params("arbitrary"),
        name="inproj",
    )(h, g, w, wt, wup, bg)


def _gelu_tanh(x):
    c = math.sqrt(2.0 / math.pi)
    return 0.5 * x * (1.0 + jnp.tanh(c * (x + 0.044715 * (x * x * x))))


def _s5_kernel(u_ref, perm_ref, permt_ref, bq_ref, cq_ref, ar_ref, ai_ref, d_ref, wglu_ref,
               o_ref, buf_ref, st_ref, bmat_ref, cmat_ref):
    @pl.when(pl.program_id(0) == 0)
    def _():
        st_ref[...] = jnp.zeros_like(st_ref)
        bmat_ref[...] = jnp.zeros_like(bmat_ref)
        cmat_ref[...] = jnp.zeros_like(cmat_ref)
        for kt in range(2):
            for ri in range(2):
                for gl in range(S5_HALF_GROUPS):
                    col = ri * S5_HALF_COLS + (gl // 2) * 128
                    bmat_ref[kt, gl * S5_GROUP:(gl + 1) * S5_GROUP, col:col + 128] = (
                        bq_ref[kt, ri, gl])
                    row = ri * S5_HALF_COLS + gl * S5_STATE
                    cmat_ref[kt, row:row + S5_STATE, (gl // 8) * 128:(gl // 8 + 1) * 128] = (
                        cq_ref[kt, ri, gl])

    ut = jnp.concatenate(
        [_dot(perm_ref[...], u_ref[:, s * S5_SUB_T:(s + 1) * S5_SUB_T, :].reshape(
            S5_SUB_ROWS, S5_WIDTH)) for s in range(S5_SUBS)], axis=0)
    ut_bf = ut.astype(BF16)

    half_w = 2 * S5_HALF_COLS
    for kt in range(2):
        buf_ref[:, kt * half_w:(kt + 1) * half_w] = _dot(
            ut_bf[:, kt * 256:(kt + 1) * 256], bmat_ref[kt])

    for kt in range(2):
        c_re = kt * half_w
        c_im = c_re + S5_HALF_COLS
        re_cols = slice(c_re, c_re + S5_HALF_COLS)
        im_cols = slice(c_im, c_im + S5_HALF_COLS)
        ar = ar_ref[:, kt * S5_HALF_COLS:(kt + 1) * S5_HALF_COLS]
        ai = ai_ref[:, kt * S5_HALF_COLS:(kt + 1) * S5_HALF_COLS]
        s_re, s_im = st_ref[:, re_cols], st_ref[:, im_cols]
        for t in range(S5_T):
            rows = slice(t * 8, (t + 1) * 8)
            s_re, s_im = (ar * s_re - ai * s_im + buf_ref[rows, re_cols],
                          ar * s_im + ai * s_re + buf_ref[rows, im_cols])
            buf_ref[rows, re_cols] = s_re
            buf_ref[rows, im_cols] = s_im
        st_ref[:, re_cols] = s_re
        st_ref[:, im_cols] = s_im

    ys = [_dot(buf_ref[:, kt * half_w:(kt + 1) * half_w].astype(BF16), cmat_ref[kt])
          for kt in range(2)]
    y = jnp.concatenate(ys, axis=-1) + d_ref[...] * ut
    y = _gelu_tanh(y)
    z = _dot(y.astype(BF16), wglu_ref[...])
    out = (z[:, :S5_WIDTH] * jax.nn.sigmoid(z[:, S5_WIDTH:])).astype(BF16)
    for s in range(S5_SUBS):
        o_ref[:, s * S5_SUB_T:(s + 1) * S5_SUB_T, :] = _dot(
            permt_ref[...], out[s * S5_SUB_ROWS:(s + 1) * S5_SUB_ROWS, :]
        ).astype(BF16).reshape(8, S5_SUB_T, S5_WIDTH)


def _s5(u, layer, perm, permt, bq, cq, ar, ai, d, wglu):
    bsz, seq, _ = u.shape
    blk = lambda j: (0, j, 0)
    return pl.pallas_call(
        _s5_kernel,
        grid=(seq // S5_T,),
        in_specs=[
            pl.BlockSpec((bsz, S5_T, S5_WIDTH), blk),
            _const_spec(perm.shape), _const_spec(permt.shape),
            _layer_spec(bq, layer), _layer_spec(cq, layer),
            _layer_spec(ar, layer), _layer_spec(ai, layer),
            _layer_spec(d, layer), _layer_spec(wglu, layer),
        ],
        out_specs=pl.BlockSpec((bsz, S5_T, S5_WIDTH), blk),
        out_shape=jax.ShapeDtypeStruct((bsz, seq, S5_WIDTH), BF16),
        scratch_shapes=[
            pltpu.VMEM((S5_ROWS, 4 * S5_HALF_COLS), F32),
            pltpu.VMEM((8, 4 * S5_HALF_COLS), F32),
            pltpu.VMEM((2, 256, 2 * S5_HALF_COLS), BF16),
            pltpu.VMEM((2, 2 * S5_HALF_COLS, 256), BF16),
        ],
        compiler_params=_params("arbitrary"),
        name="s5",
    )(u, perm, permt, bq, cq, ar, ai, d, wglu)


def _s5_params(a_re, a_im, log_dt, b_re, b_im, c_re, c_im):
    depth = a_re.shape[0]
    dt = jnp.exp(log_dt)[..., None]
    x = a_re * dt
    y = a_im * dt
    ex = jnp.exp(x)
    abar_re = ex * jnp.cos(y)
    abar_im = ex * jnp.sin(y)
    m1_re = jnp.expm1(x) * jnp.cos(y) - 2.0 * jnp.sin(0.5 * y) ** 2
    den = a_re * a_re + a_im * a_im
    coef_re = (m1_re * a_re + abar_im * a_im) / den
    coef_im = (abar_im * a_re - m1_re * a_im) / den
    bb_re = coef_re[..., None] * b_re - coef_im[..., None] * b_im
    bb_im = coef_re[..., None] * b_im + coef_im[..., None] * b_re

    grp = np.arange(S5_GROUPS)
    half_sel = jnp.asarray(grp[:, None] % 2 == np.arange(2)[None, :], F32)
    oct_sel = jnp.asarray(grp[:, None] % 8 == np.arange(8)[None, :], F32)

    def place_b(bb):
        t = jnp.einsum('lgpc,gh->lgchp', bb, half_sel)
        return t.reshape(depth, 2, S5_HALF_GROUPS, S5_GROUP, 128)

    def place_c(cc):
        t = jnp.einsum('lgcp,go->lgpoc', cc, oct_sel)
        return t.reshape(depth, 2, S5_HALF_GROUPS, S5_STATE, 128)

    bmat = jnp.stack([place_b(bb_re), place_b(bb_im)], axis=2).astype(BF16)
    cmat = jnp.stack([place_c(c_re), place_c(-c_im)], axis=2).astype(BF16)
    cols = S5_GROUPS * S5_STATE
    ar = jnp.broadcast_to(abar_re.reshape(depth, 1, cols), (depth, 8, cols))
    ai = jnp.broadcast_to(abar_im.reshape(depth, 1, cols), (depth, 8, cols))
    return bmat, cmat, ar, ai


def _gla_kernel(q_ref, kt_ref, lat_ref, v_ref, sg_ref, later_ref, hng_ref, o_ref,
                s_ref, kv_ref, sb_ref):
    @pl.when(pl.program_id(1) == 0)
    def _():
        s_ref[...] = jnp.zeros_like(s_ref)

    later = later_ref[...]
    decays = []
    for grp in range(GLA_T // GLA_CUM_T):
        cols = slice(grp * GLA_CUM_T, (grp + 1) * GLA_CUM_T)
        la = lat_ref[:, cols]
        la_hi, la_lo = _split_bf16(la)
        rest = _dot(la_hi, later) + _dot(la_lo, later)
        k_end_t = (kt_ref[:, cols] * jnp.exp(rest)).astype(BF16)
        for cc in range(GLA_CUM_T // CHUNK):
            c = grp * (GLA_CUM_T // CHUNK) + cc
            first = slice(cc * CHUNK, cc * CHUNK + 1)
            rows = slice(c * CHUNK, (c + 1) * CHUNK)
            decays.append(jnp.exp(rest[:, first] + la[:, first]))
            for h in range(GLA_HEADS):
                kv_ref[c, h] = _dot(
                    k_end_t[h * GLA_DK:(h + 1) * GLA_DK, cc * CHUNK:(cc + 1) * CHUNK],
                    v_ref[rows, h * GLA_DV:(h + 1) * GLA_DV])

    states = [s_ref[h] for h in range(GLA_HEADS)]
    for c in range(GLA_NCH):
        for h in range(GLA_HEADS):
            states[h] = states[h] * decays[c][h * GLA_DK:(h + 1) * GLA_DK, :] + kv_ref[c, h]
            sb_ref[c, h] = states[h].astype(BF16)
    for h in range(GLA_HEADS):
        s_ref[h] = states[h]

    for c in range(GLA_NCH):
        rows = slice(c * CHUNK, (c + 1) * CHUNK)
        for h in range(GLA_HEADS):
            vs = slice(h * GLA_DV, (h + 1) * GLA_DV)
            o = _dot(q_ref[rows, h * GLA_DK:(h + 1) * GLA_DK], sb_ref[c, h])
            o = o * _inv_rms(o) * hng_ref[:, vs] * sg_ref[rows, vs].astype(F32)
            o_ref[rows, vs] = o.astype(BF16)


def _gla(q, kt, lat, v, sg, layer, later, hng, bsz):
    n = q.shape[0]
    nj = n // bsz // GLA_T
    row = lambda b, j: (b * nj + j, 0)
    col = lambda b, j: (0, b * nj + j)
    return pl.pallas_call(
        _gla_kernel,
        grid=(bsz, nj),
        in_specs=[
            pl.BlockSpec((GLA_T, GLA_KEY), row),
            pl.BlockSpec((GLA_KEY, GLA_T), col),
            pl.BlockSpec((GLA_KEY, GLA_T), col),
            pl.BlockSpec((GLA_T, GLA_VAL), row),
            pl.BlockSpec((GLA_T, GLA_VAL), row),
            _const_spec(later.shape), _layer_spec(hng, layer),
        ],
        out_specs=pl.BlockSpec((GLA_T, GLA_VAL), row),
        out_shape=jax.ShapeDtypeStruct((n, GLA_VAL), BF16),
        scratch_shapes=[
            pltpu.VMEM((GLA_HEADS, GLA_DK, GLA_DV), F32),
            pltpu.VMEM((GLA_NCH, GLA_HEADS, GLA_DK, GLA_DV), F32),
            pltpu.VMEM((GLA_NCH, GLA_HEADS, GLA_DK, GLA_DV), BF16),
        ],
        compiler_params=_params("arbitrary", "arbitrary"),
        name="gla",
    )(q, kt, lat, v, sg, later, hng)


def _mix_ffn_kernel(h_ref, ys5_ref, ygla_ref, ss5_ref, sgla_ref, wbs5_ref, wbgla_ref, wout_ref,
                    g_ref, wg_ref, wu_ref, wd_ref, fg_ref, o_ref, *, final_norm):
    mixed = (ss5_ref[...].astype(F32) * _dot(ys5_ref[...], wbs5_ref[...])
             + sgla_ref[...].astype(F32) * _dot(ygla_ref[...], wbgla_ref[...]))
    h = h_ref[...] + _dot(mixed.astype(BF16), wout_ref[...])
    hg = (h * g_ref[...]).astype(BF16)
    inv = _inv_rms(h)
    acc = h
    for a, b in FF_CHUNKS:
        gate = _dot(hg, wg_ref[:, a:b]) * inv
        up = _dot(hg, wu_ref[:, a:b]) * inv
        act = (gate * jax.nn.sigmoid(gate) * up).astype(BF16)
        acc = acc + _dot(act, wd_ref[a:b, :])
    if final_norm:
        acc = acc * _inv_rms(acc) * fg_ref[...]
    o_ref[...] = acc


def _mix_ffn(h, ys5, ygla, ss5, sgla, layer, wbs5, wbgla, wout, g, wg, wu, wd, fg, final_norm):
    n = h.shape[0]
    row = lambda i: (i, 0)
    return pl.pallas_call(
        functools.partial(_mix_ffn_kernel, final_norm=final_norm),
        grid=(n // ROW_TILE,),
        in_specs=[
            pl.BlockSpec((ROW_TILE, D_MODEL), row),
            pl.BlockSpec((ROW_TILE, S5_WIDTH), row),
            pl.BlockSpec((ROW_TILE, GLA_VAL), row),
            pl.BlockSpec((ROW_TILE, D_MODEL), row),
            pl.BlockSpec((ROW_TILE, D_MODEL), row),
            _layer_spec(wbs5, layer), _layer_spec(wbgla, layer), _layer_spec(wout, layer),
            _layer_spec(g, layer), _layer_spec(wg, layer), _layer_spec(wu, layer),
            _layer_spec(wd, layer), _const_spec(fg.shape),
        ],
        out_specs=pl.BlockSpec((ROW_TILE, D_MODEL), row),
        out_shape=jax.ShapeDtypeStruct((n, D_MODEL), F32),
        compiler_params=_params("arbitrary"),
        name="mix_ffn_final" if final_norm else "mix_ffn",
    )(h, ys5, ygla, ss5, sgla, wbs5, wbgla, wout, g, wg, wu, wd, fg)


def _pack_w_in(w):
    w = w.astype(BF16)
    a_low = w[..., _W_MAIN_COLS:_W_MAIN_COLS + GLA_GATE_RANK]
    gates = w[..., _W_MAIN_COLS + GLA_GATE_RANK:]
    pad = jnp.zeros(w.shape[:-1] + (_ALOW_PAD - GLA_GATE_RANK,), BF16)
    return w, jnp.concatenate([gates, a_low, pad], axis=-1)


def _perm_matrices():
    p = np.zeros((S5_SUB_ROWS, S5_SUB_ROWS), np.float32)
    t, b = np.meshgrid(np.arange(S5_SUB_T), np.arange(8), indexing='ij')
    p[(t * 8 + b).ravel(), (b * S5_SUB_T + t).ravel()] = 1.0
    return jnp.asarray(p, BF16), jnp.asarray(p.T, BF16)


def kernel(x, attn_norm_g, w_in, s5_a_re, s5_a_im, s5_log_dt, s5_b_re, s5_b_im, s5_c_re, s5_c_im,
           s5_d, s5_w_glu, gla_w_gate_up, gla_b_gate, gla_head_norm_g, w_branch_s5, w_branch_gla,
           w_out, ffn_norm_g, w_ffn_gate, w_ffn_up, w_ffn_down, final_norm_g):
    bsz, seq, _ = x.shape
    n = bsz * seq
    depth = w_in.shape[0]
    perm, permt = _perm_matrices()
    tok = np.arange(GLA_CUM_T)
    later = jnp.asarray((tok[:, None] > tok[None, :])
                        & (tok[:, None] // CHUNK == tok[None, :] // CHUNK), BF16)
    fg = final_norm_g.reshape(1, D_MODEL)

    vec = lambda t: t.reshape(depth, 1, -1)
    w_main, w_tail = _pack_w_in(w_in)
    wup = jnp.pad(gla_w_gate_up,
                  ((0, 0), (0, _ALOW_PAD - GLA_GATE_RANK), (0, 0))).astype(BF16)
    bmat, cmat, ar, ai = _s5_params(s5_a_re, s5_a_im, s5_log_dt, s5_b_re, s5_b_im,
                                    s5_c_re, s5_c_im)
    wglu = s5_w_glu.astype(BF16)
    wbs5, wbgla, wout = (w_branch_s5.astype(BF16), w_branch_gla.astype(BF16),
                         w_out.astype(BF16))
    wg, wu, wd = w_ffn_gate.astype(BF16), w_ffn_up.astype(BF16), w_ffn_down.astype(BF16)
    attn_g, ffn_g, bgate, hng, s5d = (vec(attn_norm_g), vec(ffn_norm_g), vec(gla_b_gate),
                                      vec(gla_head_norm_g), vec(s5_d))

    h = x.reshape(n, D_MODEL)
    for l in range(depth):
        u, q, kt, lat, v, sg, ss5, sgla = _inproj(h, l, attn_g, w_main, w_tail, wup, bgate)
        ys5 = _s5(u.reshape(bsz, seq, S5_WIDTH), l, perm, permt, bmat, cmat, ar, ai,
                  s5d, wglu).reshape(n, S5_WIDTH)
        ygla = _gla(q, kt, lat, v, sg, l, later, hng, bsz)
        h = _mix_ffn(h, ys5, ygla, ss5, sgla, l, wbs5, wbgla, wout, ffn_g, wg, wu, wd, fg,
                     final_norm=(l == depth - 1))
    return h.reshape(bsz, seq, D_MODEL)
```

```python
import functools
import math

import numpy as np
import jax
import jax.numpy as jnp
from jax import lax
from jax.experimental import pallas as pl
from jax.experimental.pallas import tpu as pltpu

F32 = jnp.float32
BF16 = jnp.bfloat16

D_MODEL = 1024
EPS = 1e-6
CHUNK = 64

S5_WIDTH = 512
S5_GROUP = 16
S5_GROUPS = 32
S5_STATE = 64
S5_PARTS = 4
S5_PART_GROUPS = S5_GROUPS // S5_PARTS
S5_PART_CH = S5_PART_GROUPS * S5_GROUP
S5_PART_COLS = S5_PART_GROUPS * S5_STATE
S5_COLS = 2 * S5_GROUPS * S5_STATE

GLA_HEADS = 4
GLA_KEY = 512
GLA_VAL = 1024
GLA_DK = 128
GLA_DV = 256
GLA_GATE_RANK = 16
GLA_GATE_TEMP = 16.0

D_FF = 2816
FF_CHUNKS = ((0, 1024), (1024, 2048), (2048, 2816))

_OFF_U, _OFF_Q, _OFF_K, _OFF_V, _OFF_G, _W_MAIN_COLS = 0, 512, 1024, 1536, 2560, 3584
_OFF_GS5, _OFF_GGLA, _OFF_ALOW, _W_TAIL_COLS = 0, 1024, 2048, 2176
_ALOW_PAD = _W_TAIL_COLS - _OFF_ALOW

ROW_TILE = 512
S5_SUB_T = 64
S5_SUB_ROWS = 8 * S5_SUB_T
S5_SUBS = 2
S5_T = S5_SUBS * S5_SUB_T
S5_ROWS = 8 * S5_T
GLA_T = 1024
GLA_NCH = GLA_T // CHUNK
GLA_CUM_T = 256

VMEM_LIMIT = 56 * 1024 * 1024


def _dot(a, b):
    return jnp.dot(a, b, preferred_element_type=F32)


def _split_bf16(x):
    hi = x.astype(BF16)
    lo = (x - hi.astype(F32)).astype(BF16)
    return hi, lo


def _inv_rms(x):
    return lax.rsqrt(jnp.mean(x * x, axis=-1, keepdims=True) + EPS)


def _const_spec(shape):
    zeros = (0,) * len(shape)
    return pl.BlockSpec(shape, lambda *_: zeros, pipeline_mode=pl.Buffered(1))


def _layer_spec(arr, layer):
    idx = (layer,) + (0,) * (arr.ndim - 1)
    return pl.BlockSpec((None,) + arr.shape[1:], lambda *_: idx, pipeline_mode=pl.Buffered(1))


def _params(*semantics):
    return pltpu.CompilerParams(dimension_semantics=semantics, vmem_limit_bytes=VMEM_LIMIT)


def _inproj_kernel(x_ref, g_ref, w_ref, wt_ref, wup_ref, bg_ref,
                   u_ref, q_ref, kt_ref, lat_ref, v_ref, sg_ref, ss5_ref, sgla_ref):
    x = x_ref[...]
    xg = (x * g_ref[...]).astype(BF16)
    inv = _inv_rms(x)

    def seg(ref, a, b):
        return _dot(xg, ref[:, a:b]) * inv

    a_low = seg(wt_ref, _OFF_ALOW, _W_TAIL_COLS)
    z = _dot(a_low.astype(BF16), wup_ref[...]) + bg_ref[...]
    log_sig = jnp.minimum(z, 0.0) - jnp.log1p(jnp.exp(-jnp.abs(z)))
    lat_ref[...] = (log_sig * (1.0 / GLA_GATE_TEMP)).T
    kt_ref[...] = seg(w_ref, _OFF_K, _OFF_V).T
    u_ref[...] = seg(w_ref, _OFF_U, _OFF_Q).astype(BF16)
    q_ref[...] = (seg(w_ref, _OFF_Q, _OFF_K) * (GLA_DK ** -0.5)).astype(BF16)
    v_ref[...] = seg(w_ref, _OFF_V, _OFF_G).astype(BF16)
    g = seg(w_ref, _OFF_G, _W_MAIN_COLS)
    sg_ref[...] = (g * jax.nn.sigmoid(g)).astype(BF16)
    ss5_ref[...] = jax.nn.sigmoid(seg(wt_ref, _OFF_GS5, _OFF_GGLA)).astype(BF16)
    sgla_ref[...] = jax.nn.sigmoid(seg(wt_ref, _OFF_GGLA, _OFF_ALOW)).astype(BF16)


def _inproj(h, layer, g, w, wt, wup, bg):
    n = h.shape[0]
    row = lambda i: (i, 0)
    col = lambda i: (0, i)
    outs = [
        ((n, S5_WIDTH), BF16, (ROW_TILE, S5_WIDTH), row),
        ((n, GLA_KEY), BF16, (ROW_TILE, GLA_KEY), row),
        ((GLA_KEY, n), F32, (GLA_KEY, ROW_TILE), col),
        ((GLA_KEY, n), F32, (GLA_KEY, ROW_TILE), col),
        ((n, GLA_VAL), BF16, (ROW_TILE, GLA_VAL), row),
        ((n, GLA_VAL), BF16, (ROW_TILE, GLA_VAL), row),
        ((n, D_MODEL), BF16, (ROW_TILE, D_MODEL), row),
        ((n, D_MODEL), BF16, (ROW_TILE, D_MODEL), row)]
    return pl.pallas_call(
        _inproj_kernel,
        grid=(n // ROW_TILE,),
        in_specs=[
            pl.BlockSpec((ROW_TILE, D_MODEL), row),
            _layer_spec(g, layer),
            pl.BlockSpec((None, D_MODEL, _W_MAIN_COLS), lambda i: (layer, 0, 0),
                         pipeline_mode=pl.Buffered(1)),
            _layer_spec(wt, layer), _layer_spec(wup, layer), _layer_spec(bg, layer),
        ],
        out_specs=[pl.BlockSpec(blk, imap) for _, _, blk, imap in outs],
        out_shape=[jax.ShapeDtypeStruct(shape, dt) for shape, dt, _, _ in outs],
        compiler_params=_params("arbitrary"),
        name="inproj",
    )(h, g, w, wt, wup, bg)


def _gelu_tanh(x):
    c = math.sqrt(2.0 / math.pi)
    return 0.5 * x * (1.0 + jnp.tanh(c * (x + 0.044715 * (x * x * x))))


def _s5_kernel(u_ref, perm_ref, permt_ref, bq_ref, cq_ref, ar_ref, ai_ref, d_ref, wglu_ref,
               o_ref, buf_ref, st_ref, bmat_ref, cmat_ref):
    @pl.when(pl.program_id(0) == 0)
    def _():
        st_ref[...] = jnp.zeros_like(st_ref)
        bmat_ref[...] = jnp.zeros_like(bmat_ref)
        cmat_ref[...] = jnp.zeros_like(cmat_ref)
        for q in range(S5_PARTS):
            for ri in range(2):
                for gl in range(S5_PART_GROUPS):
                    col = ri * S5_PART_COLS + (gl // 2) * 128
                    bmat_ref[q, gl * S5_GROUP:(gl + 1) * S5_GROUP, col:col + 128] = (
                        bq_ref[q, ri, gl])
                    row = ri * S5_PART_COLS + gl * S5_STATE
                    cmat_ref[q, row:row + S5_STATE, :] = cq_ref[q, ri, gl]

    ut = jnp.concatenate(
        [_dot(perm_ref[...], u_ref[:, s * S5_SUB_T:(s + 1) * S5_SUB_T, :].reshape(
            S5_SUB_ROWS, S5_WIDTH)) for s in range(S5_SUBS)], axis=0)
    ut_bf = ut.astype(BF16)

    part_w = 2 * S5_PART_COLS
    for q in range(S5_PARTS):
        buf_ref[:, q * part_w:(q + 1) * part_w] = _dot(
            ut_bf[:, q * S5_PART_CH:(q + 1) * S5_PART_CH], bmat_ref[q])

    for q in range(S5_PARTS):
        re_cols = slice(q * part_w, q * part_w + S5_PART_COLS)
        im_cols = slice(q * part_w + S5_PART_COLS, (q + 1) * part_w)
        ar = ar_ref[:, q * S5_PART_COLS:(q + 1) * S5_PART_COLS]
        ai = ai_ref[:, q * S5_PART_COLS:(q + 1) * S5_PART_COLS]
        s_re, s_im = st_ref[:, re_cols], st_ref[:, im_cols]
        for t in range(S5_T):
            rows = slice(t * 8, (t + 1) * 8)
            s_re, s_im = (ar * s_re - ai * s_im + buf_ref[rows, re_cols],
                          ar * s_im + ai * s_re + buf_ref[rows, im_cols])
            buf_ref[rows, re_cols] = s_re
            buf_ref[rows, im_cols] = s_im
        st_ref[:, re_cols] = s_re
        st_ref[:, im_cols] = s_im

    ys = [_dot(buf_ref[:, q * part_w:(q + 1) * part_w].astype(BF16), cmat_ref[q])
          for q in range(S5_PARTS)]
    y = jnp.concatenate(ys, axis=-1) + d_ref[...] * ut
    y = _gelu_tanh(y)
    z = _dot(y.astype(BF16), wglu_ref[...])
    out = (z[:, :S5_WIDTH] * jax.nn.sigmoid(z[:, S5_WIDTH:])).astype(BF16)
    for s in range(S5_SUBS):
        o_ref[:, s * S5_SUB_T:(s + 1) * S5_SUB_T, :] = _dot(
            permt_ref[...], out[s * S5_SUB_ROWS:(s + 1) * S5_SUB_ROWS, :]
        ).astype(BF16).reshape(8, S5_SUB_T, S5_WIDTH)


def _s5(u, layer, perm, permt, bq, cq, ar, ai, d, wglu):
    bsz, seq, _ = u.shape
    blk = lambda j: (0, j, 0)
    return pl.pallas_call(
        _s5_kernel,
        grid=(seq // S5_T,),
        in_specs=[
            pl.BlockSpec((bsz, S5_T, S5_WIDTH), blk),
            _const_spec(perm.shape), _const_spec(permt.shape),
            _layer_spec(bq, layer), _layer_spec(cq, layer),
            _layer_spec(ar, layer), _layer_spec(ai, layer),
            _layer_spec(d, layer), _layer_spec(wglu, layer),
        ],
        out_specs=pl.BlockSpec((bsz, S5_T, S5_WIDTH), blk),
        out_shape=jax.ShapeDtypeStruct((bsz, seq, S5_WIDTH), BF16),
        scratch_shapes=[
            pltpu.VMEM((S5_ROWS, S5_COLS), F32),
            pltpu.VMEM((8, S5_COLS), F32),
            pltpu.VMEM((S5_PARTS, S5_PART_CH, 2 * S5_PART_COLS), BF16),
            pltpu.VMEM((S5_PARTS, 2 * S5_PART_COLS, S5_PART_CH), BF16),
        ],
        compiler_params=_params("arbitrary"),
        name="s5",
    )(u, perm, permt, bq, cq, ar, ai, d, wglu)


def _s5_params(a_re, a_im, log_dt, b_re, b_im, c_re, c_im):
    depth = a_re.shape[0]
    dt = jnp.exp(log_dt)[..., None]
    x = a_re * dt
    y = a_im * dt
    ex = jnp.exp(x)
    abar_re = ex * jnp.cos(y)
    abar_im = ex * jnp.sin(y)
    m1_re = jnp.expm1(x) * jnp.cos(y) - 2.0 * jnp.sin(0.5 * y) ** 2
    den = a_re * a_re + a_im * a_im
    coef_re = (m1_re * a_re + abar_im * a_im) / den
    coef_im = (abar_im * a_re - m1_re * a_im) / den
    bb_re = coef_re[..., None] * b_re - coef_im[..., None] * b_im
    bb_im = coef_re[..., None] * b_im + coef_im[..., None] * b_re

    grp = np.arange(S5_GROUPS)
    half_sel = jnp.asarray(grp[:, None] % 2 == np.arange(2)[None, :], F32)
    oct_sel = jnp.asarray(grp[:, None] % 8 == np.arange(8)[None, :], F32)

    def place_b(bb):
        t = jnp.einsum('lgpc,gh->lgchp', bb, half_sel)
        return t.reshape(depth, S5_PARTS, S5_PART_GROUPS, S5_GROUP, 128)

    def place_c(cc):
        t = jnp.einsum('lgcp,go->lgpoc', cc, oct_sel)
        return t.reshape(depth, S5_PARTS, S5_PART_GROUPS, S5_STATE, 128)

    bmat = jnp.stack([place_b(bb_re), place_b(bb_im)], axis=2).astype(BF16)
    cmat = jnp.stack([place_c(c_re), place_c(-c_im)], axis=2).astype(BF16)
    cols = S5_GROUPS * S5_STATE
    ar = jnp.broadcast_to(abar_re.reshape(depth, 1, cols), (depth, 8, cols))
    ai = jnp.broadcast_to(abar_im.reshape(depth, 1, cols), (depth, 8, cols))
    return bmat, cmat, ar, ai


def _gla_kernel(q_ref, kt_ref, lat_ref, v_ref, sg_ref, later_ref, hng_ref, o_ref,
                s_ref, sb_ref):
    @pl.when(pl.program_id(1) == 0)
    def _():
        s_ref[...] = jnp.zeros_like(s_ref)

    later = later_ref[...]

    def emit_output(c):
        rows = slice(c * CHUNK, (c + 1) * CHUNK)
        for h in range(GLA_HEADS):
            vs = slice(h * GLA_DV, (h + 1) * GLA_DV)
            o = _dot(q_ref[rows, h * GLA_DK:(h + 1) * GLA_DK], sb_ref[c, h])
            o = o * _inv_rms(o) * hng_ref[:, vs] * sg_ref[rows, vs].astype(F32)
            o_ref[rows, vs] = o.astype(BF16)

    states = [s_ref[h] for h in range(GLA_HEADS)]
    for grp in range(GLA_T // GLA_CUM_T):
        cols = slice(grp * GLA_CUM_T, (grp + 1) * GLA_CUM_T)
        la = lat_ref[:, cols]
        la_hi, la_lo = _split_bf16(la)
        rest = _dot(la_hi, later) + _dot(la_lo, later)
        k_end_t = (kt_ref[:, cols] * jnp.exp(rest)).astype(BF16)
        for cc in range(GLA_CUM_T // CHUNK):
            c = grp * (GLA_CUM_T // CHUNK) + cc
            first = slice(cc * CHUNK, cc * CHUNK + 1)
            rows = slice(c * CHUNK, (c + 1) * CHUNK)
            decay = jnp.exp(rest[:, first] + la[:, first])
            for h in range(GLA_HEADS):
                ks = slice(h * GLA_DK, (h + 1) * GLA_DK)
                kv = _dot(k_end_t[ks, cc * CHUNK:(cc + 1) * CHUNK],
                          v_ref[rows, h * GLA_DV:(h + 1) * GLA_DV])
                states[h] = states[h] * decay[ks, :] + kv
                sb_ref[c, h] = states[h].astype(BF16)
            if c >= 1:
                emit_output(c - 1)
    emit_output(GLA_NCH - 1)
    for h in range(GLA_HEADS):
        s_ref[h] = states[h]


def _gla(q, kt, lat, v, sg, layer, later, hng, bsz):
    n = q.shape[0]
    nj = n // bsz // GLA_T
    row = lambda b, j: (b * nj + j, 0)
    col = lambda b, j: (0, b * nj + j)
    return pl.pallas_call(
        _gla_kernel,
        grid=(bsz, nj),
        in_specs=[
            pl.BlockSpec((GLA_T, GLA_KEY), row),
            pl.BlockSpec((GLA_KEY, GLA_T), col),
            pl.BlockSpec((GLA_KEY, GLA_T), col),
            pl.BlockSpec((GLA_T, GLA_VAL), row),
            pl.BlockSpec((GLA_T, GLA_VAL), row),
            _const_spec(later.shape), _layer_spec(hng, layer),
        ],
        out_specs=pl.BlockSpec((GLA_T, GLA_VAL), row),
        out_shape=jax.ShapeDtypeStruct((n, GLA_VAL), BF16),
        scratch_shapes=[
            pltpu.VMEM((GLA_HEADS, GLA_DK, GLA_DV), F32),
            pltpu.VMEM((GLA_NCH, GLA_HEADS, GLA_DK, GLA_DV), BF16),
        ],
        compiler_params=_params("arbitrary", "arbitrary"),
        name="gla",
    )(q, kt, lat, v, sg, later, hng)


def _mix_ffn_kernel(h_ref, ys5_ref, ygla_ref, ss5_ref, sgla_ref, wbs5_ref, wbgla_ref, wout_ref,
                    g_ref, wg_ref, wu_ref, wd_ref, fg_ref, o_ref, *, final_norm):
    mixed = (ss5_ref[...].astype(F32) * _dot(ys5_ref[...], wbs5_ref[...])
             + sgla_ref[...].astype(F32) * _dot(ygla_ref[...], wbgla_ref[...]))
    h = h_ref[...] + _dot(mixed.astype(BF16), wout_ref[...])
    hg = (h * g_ref[...]).astype(BF16)
    inv = _inv_rms(h)
    acc = h
    for a, b in FF_CHUNKS:
        gate = _dot(hg, wg_ref[:, a:b]) * inv
        up = _dot(hg, wu_ref[:, a:b]) * inv
        act = (gate * jax.nn.sigmoid(gate) * up).astype(BF16)
        acc = acc + _dot(act, wd_ref[a:b, :])
    if final_norm:
        acc = acc * _inv_rms(acc) * fg_ref[...]
    o_ref[...] = acc


def _mix_ffn(h, ys5, ygla, ss5, sgla, layer, wbs5, wbgla, wout, g, wg, wu, wd, fg, final_norm):
    n = h.shape[0]
    row = lambda i: (i, 0)
    return pl.pallas_call(
        functools.partial(_mix_ffn_kernel, final_norm=final_norm),
        grid=(n // ROW_TILE,),
        in_specs=[
            pl.BlockSpec((ROW_TILE, D_MODEL), row),
            pl.BlockSpec((ROW_TILE, S5_WIDTH), row),
            pl.BlockSpec((ROW_TILE, GLA_VAL), row),
            pl.BlockSpec((ROW_TILE, D_MODEL), row),
            pl.BlockSpec((ROW_TILE, D_MODEL), row),
            _layer_spec(wbs5, layer), _layer_spec(wbgla, layer), _layer_spec(wout, layer),
            _layer_spec(g, layer), _layer_spec(wg, layer), _layer_spec(wu, layer),
            _layer_spec(wd, layer), _const_spec(fg.shape),
        ],
        out_specs=pl.BlockSpec((ROW_TILE, D_MODEL), row),
        out_shape=jax.ShapeDtypeStruct((n, D_MODEL), F32),
        compiler_params=_params("arbitrary"),
        name="mix_ffn_final" if final_norm else "mix_ffn",
    )(h, ys5, ygla, ss5, sgla, wbs5, wbgla, wout, g, wg, wu, wd, fg)


def _pack_w_in(w):
    w = w.astype(BF16)
    a_low = w[..., _W_MAIN_COLS:_W_MAIN_COLS + GLA_GATE_RANK]
    gates = w[..., _W_MAIN_COLS + GLA_GATE_RANK:]
    pad = jnp.zeros(w.shape[:-1] + (_ALOW_PAD - GLA_GATE_RANK,), BF16)
    return w, jnp.concatenate([gates, a_low, pad], axis=-1)


def _perm_matrices():
    p = np.zeros((S5_SUB_ROWS, S5_SUB_ROWS), np.float32)
    t, b = np.meshgrid(np.arange(S5_SUB_T), np.arange(8), indexing='ij')
    p[(t * 8 + b).ravel(), (b * S5_SUB_T + t).ravel()] = 1.0
    return jnp.asarray(p, BF16), jnp.asarray(p.T, BF16)


def kernel(x, attn_norm_g, w_in, s5_a_re, s5_a_im, s5_log_dt, s5_b_re, s5_b_im, s5_c_re, s5_c_im,
           s5_d, s5_w_glu, gla_w_gate_up, gla_b_gate, gla_head_norm_g, w_branch_s5, w_branch_gla,
           w_out, ffn_norm_g, w_ffn_gate, w_ffn_up, w_ffn_down, final_norm_g):
    bsz, seq, _ = x.shape
    n = bsz * seq
    depth = w_in.shape[0]
    perm, permt = _perm_matrices()
    tok = np.arange(GLA_CUM_T)
    later = jnp.asarray((tok[:, None] > tok[None, :])
                        & (tok[:, None] // CHUNK == tok[None, :] // CHUNK), BF16)
    fg = final_norm_g.reshape(1, D_MODEL)

    vec = lambda t: t.reshape(depth, 1, -1)
    w_main, w_tail = _pack_w_in(w_in)
    wup = jnp.pad(gla_w_gate_up,
                  ((0, 0), (0, _ALOW_PAD - GLA_GATE_RANK), (0, 0))).astype(BF16)
    bmat, cmat, ar, ai = _s5_params(s5_a_re, s5_a_im, s5_log_dt, s5_b_re, s5_b_im,
                                    s5_c_re, s5_c_im)
    wglu = s5_w_glu.astype(BF16)
    wbs5, wbgla, wout = (w_branch_s5.astype(BF16), w_branch_gla.astype(BF16),
                         w_out.astype(BF16))
    wg, wu, wd = w_ffn_gate.astype(BF16), w_ffn_up.astype(BF16), w_ffn_down.astype(BF16)
    attn_g, ffn_g, bgate, hng, s5d = (vec(attn_norm_g), vec(ffn_norm_g), vec(gla_b_gate),
                                      vec(gla_head_norm_g), vec(s5_d))

    h = x.reshape(n, D_MODEL)
    for l in range(depth):
        u, q, kt, lat, v, sg, ss5, sgla = _inproj(h, l, attn_g, w_main, w_tail, wup, bgate)
        ys5 = _s5(u.reshape(bsz, seq, S5_WIDTH), l, perm, permt, bmat, cmat, ar, ai,
                  s5d, wglu).reshape(n, S5_WIDTH)
        ygla = _gla(q, kt, lat, v, sg, l, later, hng, bsz)
        h = _mix_ffn(h, ys5, ygla, ss5, sgla, l, wbs5, wbgla, wout, ffn_g, wg, wu, wd, fg,
                     final_norm=(l == depth - 1))
    return h.reshape(bsz, seq, D_MODEL)
```

```python
import functools
import math

import numpy as np
import jax
import jax.numpy as jnp
from jax import lax
from jax.experimental import pallas as pl
from jax.experimental.pallas import tpu as pltpu

F32 = jnp.float32
BF16 = jnp.bfloat16

D_MODEL = 1024
EPS = 1e-6
CHUNK = 64

S5_WIDTH = 512
S5_GROUP = 16
S5_GROUPS = 32
S5_STATE = 64
S5_PARTS = 4
S5_PART_GROUPS = S5_GROUPS // S5_PARTS
S5_PART_CH = S5_PART_GROUPS * S5_GROUP
S5_PART_COLS = S5_PART_GROUPS * S5_STATE
S5_COLS = 2 * S5_GROUPS * S5_STATE

GLA_HEADS = 4
GLA_KEY = 512
GLA_VAL = 1024
GLA_DK = 128
GLA_DV = 256
GLA_GATE_RANK = 16
GLA_GATE_TEMP = 16.0

D_FF = 2816
FF_CHUNKS = ((0, 1024), (1024, 2048), (2048, 2816))

_OFF_U, _OFF_Q, _OFF_K, _OFF_V, _OFF_G, _W_MAIN_COLS = 0, 512, 1024, 1536, 2560, 3584
_OFF_GS5, _OFF_GGLA, _OFF_ALOW, _W_TAIL_COLS = 0, 1024, 2048, 2176
_ALOW_PAD = _W_TAIL_COLS - _OFF_ALOW

ROW_TILE = 512
IN_TILE = 1024
S5_SUB_T = 64
S5_SUB_ROWS = 8 * S5_SUB_T
S5_SUBS = 2
S5_T = S5_SUBS * S5_SUB_T
S5_ROWS = 8 * S5_T
GLA_T = 1024
GLA_NCH = GLA_T // CHUNK
GLA_CUM_T = 256

VMEM_LIMIT = 56 * 1024 * 1024


def _dot(a, b):
    return jnp.dot(a, b, preferred_element_type=F32)


def _split_bf16(x):
    hi = x.astype(BF16)
    lo = (x - hi.astype(F32)).astype(BF16)
    return hi, lo


def _inv_rms(x):
    return lax.rsqrt(jnp.mean(x * x, axis=-1, keepdims=True) + EPS)


def _const_spec(shape):
    zeros = (0,) * len(shape)
    return pl.BlockSpec(shape, lambda *_: zeros, pipeline_mode=pl.Buffered(1))


def _layer_spec(arr, layer):
    idx = (layer,) + (0,) * (arr.ndim - 1)
    return pl.BlockSpec((None,) + arr.shape[1:], lambda *_: idx, pipeline_mode=pl.Buffered(1))


def _params(*semantics):
    return pltpu.CompilerParams(dimension_semantics=semantics, vmem_limit_bytes=VMEM_LIMIT)


def _inproj_kernel(x_ref, g_ref, w_ref, wt_ref, wup_ref, bg_ref,
                   u_ref, q_ref, kt_ref, lat_ref, v_ref, sg_ref, ss5_ref, sgla_ref):
    x = x_ref[...]
    xg = (x * g_ref[...]).astype(BF16)
    inv = _inv_rms(x)

    def seg(ref, a, b):
        return _dot(xg, ref[:, a:b]) * inv

    a_low = seg(wt_ref, _OFF_ALOW, _W_TAIL_COLS)
    z = _dot(a_low.astype(BF16), wup_ref[...]) + bg_ref[...]
    log_sig = jnp.minimum(z, 0.0) - jnp.log1p(jnp.exp(-jnp.abs(z)))
    lat_ref[...] = (log_sig * (1.0 / GLA_GATE_TEMP)).T
    kt_ref[...] = seg(w_ref, _OFF_K, _OFF_V).T
    u_ref[...] = seg(w_ref, _OFF_U, _OFF_Q).astype(BF16)
    q_ref[...] = (seg(w_ref, _OFF_Q, _OFF_K) * (GLA_DK ** -0.5)).astype(BF16)
    v_ref[...] = seg(w_ref, _OFF_V, _OFF_G).astype(BF16)
    g = seg(w_ref, _OFF_G, _W_MAIN_COLS)
    sg_ref[...] = (g * jax.nn.sigmoid(g)).astype(BF16)
    ss5_ref[...] = jax.nn.sigmoid(seg(wt_ref, _OFF_GS5, _OFF_GGLA)).astype(BF16)
    sgla_ref[...] = jax.nn.sigmoid(seg(wt_ref, _OFF_GGLA, _OFF_ALOW)).astype(BF16)


def _inproj(h, layer, g, w, wt, wup, bg):
    n = h.shape[0]
    row = lambda i: (i, 0)
    col = lambda i: (0, i)
    outs = [
        ((n, S5_WIDTH), BF16, (IN_TILE, S5_WIDTH), row),
        ((n, GLA_KEY), BF16, (IN_TILE, GLA_KEY), row),
        ((GLA_KEY, n), F32, (GLA_KEY, IN_TILE), col),
        ((GLA_KEY, n), F32, (GLA_KEY, IN_TILE), col),
        ((n, GLA_VAL), BF16, (IN_TILE, GLA_VAL), row),
        ((n, GLA_VAL), BF16, (IN_TILE, GLA_VAL), row),
        ((n, D_MODEL), BF16, (IN_TILE, D_MODEL), row),
        ((n, D_MODEL), BF16, (IN_TILE, D_MODEL), row)]
    return pl.pallas_call(
        _inproj_kernel,
        grid=(n // IN_TILE,),
        in_specs=[
            pl.BlockSpec((IN_TILE, D_MODEL), row),
            _layer_spec(g, layer),
            pl.BlockSpec((None, D_MODEL, _W_MAIN_COLS), lambda i: (layer, 0, 0),
                         pipeline_mode=pl.Buffered(1)),
            _layer_spec(wt, layer), _layer_spec(wup, layer), _layer_spec(bg, layer),
        ],
        out_specs=[pl.BlockSpec(blk, imap) for _, _, blk, imap in outs],
        out_shape=[jax.ShapeDtypeStruct(shape, dt) for shape, dt, _, _ in outs],
        compiler_params=_params("arbitrary"),
        name="inproj",
    )(h, g, w, wt, wup, bg)


def _gelu_tanh(x):
    c = math.sqrt(2.0 / math.pi)
    return 0.5 * x * (1.0 + jnp.tanh(c * (x + 0.044715 * (x * x * x))))


def _s5_kernel(u_ref, perm_ref, permt_ref, bq_ref, cq_ref, ar_ref, ai_ref, d_ref, wglu_ref,
               o_ref, buf_ref, st_ref, bmat_ref, cmat_ref):
    @pl.when(pl.program_id(0) == 0)
    def _():
        st_ref[...] = jnp.zeros_like(st_ref)
        bmat_ref[...] = jnp.zeros_like(bmat_ref)
        cmat_ref[...] = jnp.zeros_like(cmat_ref)
        for q in range(S5_PARTS):
            for ri in range(2):
                for gl in range(S5_PART_GROUPS):
                    col = ri * S5_PART_COLS + (gl // 2) * 128
                    bmat_ref[q, gl * S5_GROUP:(gl + 1) * S5_GROUP, col:col + 128] = (
                        bq_ref[q, ri, gl])
                    row = ri * S5_PART_COLS + gl * S5_STATE
                    cmat_ref[q, row:row + S5_STATE, :] = cq_ref[q, ri, gl]

    ut = jnp.concatenate(
        [_dot(perm_ref[...], u_ref[:, s * S5_SUB_T:(s + 1) * S5_SUB_T, :].reshape(
            S5_SUB_ROWS, S5_WIDTH)) for s in range(S5_SUBS)], axis=0)
    ut_bf = ut.astype(BF16)

    part_w = 2 * S5_PART_COLS
    for q in range(S5_PARTS):
        buf_ref[:, q * part_w:(q + 1) * part_w] = _dot(
            ut_bf[:, q * S5_PART_CH:(q + 1) * S5_PART_CH], bmat_ref[q])

    for q in range(S5_PARTS):
        re_cols = slice(q * part_w, q * part_w + S5_PART_COLS)
        im_cols = slice(q * part_w + S5_PART_COLS, (q + 1) * part_w)
        ar = ar_ref[:, q * S5_PART_COLS:(q + 1) * S5_PART_COLS]
        ai = ai_ref[:, q * S5_PART_COLS:(q + 1) * S5_PART_COLS]
        s_re, s_im = st_ref[:, re_cols], st_ref[:, im_cols]
        for t in range(S5_T):
            rows = slice(t * 8, (t + 1) * 8)
            s_re, s_im = (ar * s_re - ai * s_im + buf_ref[rows, re_cols],
                          ar * s_im + ai * s_re + buf_ref[rows, im_cols])
            buf_ref[rows, re_cols] = s_re
            buf_ref[rows, im_cols] = s_im
        st_ref[:, re_cols] = s_re
        st_ref[:, im_cols] = s_im

    ys = [_dot(buf_ref[:, q * part_w:(q + 1) * part_w].astype(BF16), cmat_ref[q])
          for q in range(S5_PARTS)]
    y = jnp.concatenate(ys, axis=-1) + d_ref[...] * ut
    y = _gelu_tanh(y)
    z = _dot(y.astype(BF16), wglu_ref[...])
    out = (z[:, :S5_WIDTH] * jax.nn.sigmoid(z[:, S5_WIDTH:])).astype(BF16)
    for s in range(S5_SUBS):
        o_ref[:, s * S5_SUB_T:(s + 1) * S5_SUB_T, :] = _dot(
            permt_ref[...], out[s * S5_SUB_ROWS:(s + 1) * S5_SUB_ROWS, :]
        ).astype(BF16).reshape(8, S5_SUB_T, S5_WIDTH)


def _s5(u, layer, perm, permt, bq, cq, ar, ai, d, wglu):
    bsz, seq, _ = u.shape
    blk = lambda j: (0, j, 0)
    return pl.pallas_call(
        _s5_kernel,
        grid=(seq // S5_T,),
        in_specs=[
            pl.BlockSpec((bsz, S5_T, S5_WIDTH), blk),
            _const_spec(perm.shape), _const_spec(permt.shape),
            _layer_spec(bq, layer), _layer_spec(cq, layer),
            _layer_spec(ar, layer), _layer_spec(ai, layer),
            _layer_spec(d, layer), _layer_spec(wglu, layer),
        ],
        out_specs=pl.BlockSpec((bsz, S5_T, S5_WIDTH), blk),
        out_shape=jax.ShapeDtypeStruct((bsz, seq, S5_WIDTH), BF16),
        scratch_shapes=[
            pltpu.VMEM((S5_ROWS, S5_COLS), F32),
            pltpu.VMEM((8, S5_COLS), F32),
            pltpu.VMEM((S5_PARTS, S5_PART_CH, 2 * S5_PART_COLS), BF16),
            pltpu.VMEM((S5_PARTS, 2 * S5_PART_COLS, S5_PART_CH), BF16),
        ],
        compiler_params=_params("arbitrary"),
        name="s5",
    )(u, perm, permt, bq, cq, ar, ai, d, wglu)


def _s5_params(a_re, a_im, log_dt, b_re, b_im, c_re, c_im):
    depth = a_re.shape[0]
    dt = jnp.exp(log_dt)[..., None]
    x = a_re * dt
    y = a_im * dt
    ex = jnp.exp(x)
    abar_re = ex * jnp.cos(y)
    abar_im = ex * jnp.sin(y)
    m1_re = jnp.expm1(x) * jnp.cos(y) - 2.0 * jnp.sin(0.5 * y) ** 2
    den = a_re * a_re + a_im * a_im
    coef_re = (m1_re * a_re + abar_im * a_im) / den
    coef_im = (abar_im * a_re - m1_re * a_im) / den
    bb_re = coef_re[..., None] * b_re - coef_im[..., None] * b_im
    bb_im = coef_re[..., None] * b_im + coef_im[..., None] * b_re

    grp = np.arange(S5_GROUPS)
    half_sel = jnp.asarray(grp[:, None] % 2 == np.arange(2)[None, :], F32)
    oct_sel = jnp.asarray(grp[:, None] % 8 == np.arange(8)[None, :], F32)

    def place_b(bb):
        t = jnp.einsum('lgpc,gh->lgchp', bb, half_sel)
        return t.reshape(depth, S5_PARTS, S5_PART_GROUPS, S5_GROUP, 128)

    def place_c(cc):
        t = jnp.einsum('lgcp,go->lgpoc', cc, oct_sel)
        return t.reshape(depth, S5_PARTS, S5_PART_GROUPS, S5_STATE, 128)

    bmat = jnp.stack([place_b(bb_re), place_b(bb_im)], axis=2).astype(BF16)
    cmat = jnp.stack([place_c(c_re), place_c(-c_im)], axis=2).astype(BF16)
    cols = S5_GROUPS * S5_STATE
    ar = jnp.broadcast_to(abar_re.reshape(depth, 1, cols), (depth, 8, cols))
    ai = jnp.broadcast_to(abar_im.reshape(depth, 1, cols), (depth, 8, cols))
    return bmat, cmat, ar, ai


def _gla_kernel(q_ref, kt_ref, lat_ref, v_ref, sg_ref, later_ref, hng_ref, o_ref,
                s_ref, sb_ref):
    @pl.when(pl.program_id(1) == 0)
    def _():
        s_ref[...] = jnp.zeros_like(s_ref)

    later = later_ref[...]

    def emit_output(c):
        rows = slice(c * CHUNK, (c + 1) * CHUNK)
        for h in range(GLA_HEADS):
            vs = slice(h * GLA_DV, (h + 1) * GLA_DV)
            o = _dot(q_ref[rows, h * GLA_DK:(h + 1) * GLA_DK], sb_ref[c, h])
            o = o * _inv_rms(o) * hng_ref[:, vs] * sg_ref[rows, vs].astype(F32)
            o_ref[rows, vs] = o.astype(BF16)

    states = [s_ref[h] for h in range(GLA_HEADS)]
    for grp in range(GLA_T // GLA_CUM_T):
        cols = slice(grp * GLA_CUM_T, (grp + 1) * GLA_CUM_T)
        la = lat_ref[:, cols]
        la_hi, la_lo = _split_bf16(la)
        rest = _dot(la_hi, later) + _dot(la_lo, later)
        k_end_t = (kt_ref[:, cols] * jnp.exp(rest)).astype(BF16)
        for cc in range(GLA_CUM_T // CHUNK):
            c = grp * (GLA_CUM_T // CHUNK) + cc
            first = slice(cc * CHUNK, cc * CHUNK + 1)
            rows = slice(c * CHUNK, (c + 1) * CHUNK)
            decay = jnp.exp(rest[:, first] + la[:, first])
            for h in range(GLA_HEADS):
                ks = slice(h * GLA_DK, (h + 1) * GLA_DK)
                kv = _dot(k_end_t[ks, cc * CHUNK:(cc + 1) * CHUNK],
                          v_ref[rows, h * GLA_DV:(h + 1) * GLA_DV])
                states[h] = states[h] * decay[ks, :] + kv
                sb_ref[c, h] = states[h].astype(BF16)
            if c >= 1:
                emit_output(c - 1)
    emit_output(GLA_NCH - 1)
    for h in range(GLA_HEADS):
        s_ref[h] = states[h]


def _gla(q, kt, lat, v, sg, layer, later, hng, bsz):
    n = q.shape[0]
    nj = n // bsz // GLA_T
    row = lambda b, j: (b * nj + j, 0)
    col = lambda b, j: (0, b * nj + j)
    return pl.pallas_call(
        _gla_kernel,
        grid=(bsz, nj),
        in_specs=[
            pl.BlockSpec((GLA_T, GLA_KEY), row),
            pl.BlockSpec((GLA_KEY, GLA_T), col),
            pl.BlockSpec((GLA_KEY, GLA_T), col),
            pl.BlockSpec((GLA_T, GLA_VAL), row),
            pl.BlockSpec((GLA_T, GLA_VAL), row),
            _const_spec(later.shape), _layer_spec(hng, layer),
        ],
        out_specs=pl.BlockSpec((GLA_T, GLA_VAL), row),
        out_shape=jax.ShapeDtypeStruct((n, GLA_VAL), BF16),
        scratch_shapes=[
            pltpu.VMEM((GLA_HEADS, GLA_DK, GLA_DV), F32),
            pltpu.VMEM((GLA_NCH, GLA_HEADS, GLA_DK, GLA_DV), BF16),
        ],
        compiler_params=_params("arbitrary", "arbitrary"),
        name="gla",
    )(q, kt, lat, v, sg, later, hng)


def _mix_ffn_kernel(h_ref, ys5_ref, ygla_ref, ss5_ref, sgla_ref, wbs5_ref, wbgla_ref, wout_ref,
                    g_ref, wg_ref, wu_ref, wd_ref, fg_ref, o_ref, *, final_norm):
    mixed = (ss5_ref[...].astype(F32) * _dot(ys5_ref[...], wbs5_ref[...])
             + sgla_ref[...].astype(F32) * _dot(ygla_ref[...], wbgla_ref[...]))
    h = h_ref[...] + _dot(mixed.astype(BF16), wout_ref[...])
    hg = (h * g_ref[...]).astype(BF16)
    inv = _inv_rms(h)
    acc = h
    for a, b in FF_CHUNKS:
        gate = _dot(hg, wg_ref[:, a:b]) * inv
        up = _dot(hg, wu_ref[:, a:b]) * inv
        act = (gate * jax.nn.sigmoid(gate) * up).astype(BF16)
        acc = acc + _dot(act, wd_ref[a:b, :])
    if final_norm:
        acc = acc * _inv_rms(acc) * fg_ref[...]
    o_ref[...] = acc


def _mix_ffn(h, ys5, ygla, ss5, sgla, layer, wbs5, wbgla, wout, g, wg, wu, wd, fg, final_norm):
    n = h.shape[0]
    row = lambda i: (i, 0)
    return pl.pallas_call(
        functools.partial(_mix_ffn_kernel, final_norm=final_norm),
        grid=(n // ROW_TILE,),
        in_specs=[
            pl.BlockSpec((ROW_TILE, D_MODEL), row),
            pl.BlockSpec((ROW_TILE, S5_WIDTH), row),
            pl.BlockSpec((ROW_TILE, GLA_VAL), row),
            pl.BlockSpec((ROW_TILE, D_MODEL), row),
            pl.BlockSpec((ROW_TILE, D_MODEL), row),
            _layer_spec(wbs5, layer), _layer_spec(wbgla, layer), _layer_spec(wout, layer),
            _layer_spec(g, layer), _layer_spec(wg, layer), _layer_spec(wu, layer),
            _layer_spec(wd, layer), _const_spec(fg.shape),
        ],
        out_specs=pl.BlockSpec((ROW_TILE, D_MODEL), row),
        out_shape=jax.ShapeDtypeStruct((n, D_MODEL), F32),
        compiler_params=_params("arbitrary"),
        name="mix_ffn_final" if final_norm else "mix_ffn",
    )(h, ys5, ygla, ss5, sgla, wbs5, wbgla, wout, g, wg, wu, wd, fg)


def _pack_w_in(w):
    w = w.astype(BF16)
    a_low = w[..., _W_MAIN_COLS:_W_MAIN_COLS + GLA_GATE_RANK]
    gates = w[..., _W_MAIN_COLS + GLA_GATE_RANK:]
    pad = jnp.zeros(w.shape[:-1] + (_ALOW_PAD - GLA_GATE_RANK,), BF16)
    return w, jnp.concatenate([gates, a_low, pad], axis=-1)


def _perm_matrices():
    p = np.zeros((S5_SUB_ROWS, S5_SUB_ROWS), np.float32)
    t, b = np.meshgrid(np.arange(S5_SUB_T), np.arange(8), indexing='ij')
    p[(t * 8 + b).ravel(), (b * S5_SUB_T + t).ravel()] = 1.0
    return jnp.asarray(p, BF16), jnp.asarray(p.T, BF16)


def kernel(x, attn_norm_g, w_in, s5_a_re, s5_a_im, s5_log_dt, s5_b_re, s5_b_im, s5_c_re, s5_c_im,
           s5_d, s5_w_glu, gla_w_gate_up, gla_b_gate, gla_head_norm_g, w_branch_s5, w_branch_gla,
           w_out, ffn_norm_g, w_ffn_gate, w_ffn_up, w_ffn_down, final_norm_g):
    bsz, seq, _ = x.shape
    n = bsz * seq
    depth = w_in.shape[0]
    perm, permt = _perm_matrices()
    tok = np.arange(GLA_CUM_T)
    later = jnp.asarray((tok[:, None] > tok[None, :])
                        & (tok[:, None] // CHUNK == tok[None, :] // CHUNK), BF16)
    fg = final_norm_g.reshape(1, D_MODEL)

    vec = lambda t: t.reshape(depth, 1, -1)
    w_main, w_tail = _pack_w_in(w_in)
    wup = jnp.pad(gla_w_gate_up,
                  ((0, 0), (0, _ALOW_PAD - GLA_GATE_RANK), (0, 0))).astype(BF16)
    bmat, cmat, ar, ai = _s5_params(s5_a_re, s5_a_im, s5_log_dt, s5_b_re, s5_b_im,
                                    s5_c_re, s5_c_im)
    wglu = s5_w_glu.astype(BF16)
    wbs5, wbgla, wout = (w_branch_s5.astype(BF16), w_branch_gla.astype(BF16),
                         w_out.astype(BF16))
    wg, wu, wd = w_ffn_gate.astype(BF16), w_ffn_up.astype(BF16), w_ffn_down.astype(BF16)
    attn_g, ffn_g, bgate, hng, s5d = (vec(attn_norm_g), vec(ffn_norm_g), vec(gla_b_gate),
                                      vec(gla_head_norm_g), vec(s5_d))

    h = x.reshape(n, D_MODEL)
    for l in range(depth):
        u, q, kt, lat, v, sg, ss5, sgla = _inproj(h, l, attn_g, w_main, w_tail, wup, bgate)
        ys5 = _s5(u.reshape(bsz, seq, S5_WIDTH), l, perm, permt, bmat, cmat, ar, ai,
                  s5d, wglu).reshape(n, S5_WIDTH)
        ygla = _gla(q, kt, lat, v, sg, l, later, hng, bsz)
        h = _mix_ffn(h, ys5, ygla, ss5, sgla, l, wbs5, wbgla, wout, ffn_g, wg, wu, wd, fg,
                     final_norm=(l == depth - 1))
    return h.reshape(bsz, seq, D_MODEL)
```

```python
import functools
import math

import numpy as np
import jax
import jax.numpy as jnp
from jax import lax
from jax.experimental import pallas as pl
from jax.experimental.pallas import tpu as pltpu

F32 = jnp.float32
BF16 = jnp.bfloat16

D_MODEL = 1024
EPS = 1e-6
CHUNK = 64

S5_WIDTH = 512
S5_GROUP = 16
S5_GROUPS = 32
S5_STATE = 64
S5_PARTS = 4
S5_PART_GROUPS = S5_GROUPS // S5_PARTS
S5_PART_CH = S5_PART_GROUPS * S5_GROUP
S5_PART_COLS = S5_PART_GROUPS * S5_STATE
S5_COLS = 2 * S5_GROUPS * S5_STATE

GLA_HEADS = 4
GLA_KEY = 512
GLA_VAL = 1024
GLA_DK = 128
GLA_DV = 256
GLA_GATE_RANK = 16
GLA_GATE_TEMP = 16.0

D_FF = 2816
FF_CHUNKS = ((0, 1024), (1024, 2048), (2048, 2816))

_OFF_U, _OFF_Q, _OFF_K, _OFF_V, _OFF_G, _W_MAIN_COLS = 0, 512, 1024, 1536, 2560, 3584
_OFF_GS5, _OFF_GGLA, _OFF_ALOW, _W_TAIL_COLS = 0, 1024, 2048, 2176
_ALOW_PAD = _W_TAIL_COLS - _OFF_ALOW

ROW_TILE = 512
IN_TILE = 1024
S5_SUB_T = 64
S5_SUB_ROWS = 8 * S5_SUB_T
S5_SUBS = 2
S5_T = S5_SUBS * S5_SUB_T
S5_ROWS = 8 * S5_T
GLA_T = 1024
GLA_NCH = GLA_T // CHUNK
GLA_CUM_T = 256

VMEM_LIMIT = 56 * 1024 * 1024


def _dot(a, b):
    return jnp.dot(a, b, preferred_element_type=F32)


def _split_bf16(x):
    hi = x.astype(BF16)
    lo = (x - hi.astype(F32)).astype(BF16)
    return hi, lo


def _inv_rms(x):
    return lax.rsqrt(jnp.mean(x * x, axis=-1, keepdims=True) + EPS)


def _const_spec(shape):
    zeros = (0,) * len(shape)
    return pl.BlockSpec(shape, lambda *_: zeros, pipeline_mode=pl.Buffered(1))


def _layer_spec(arr, layer):
    idx = (layer,) + (0,) * (arr.ndim - 1)
    return pl.BlockSpec((None,) + arr.shape[1:], lambda *_: idx, pipeline_mode=pl.Buffered(1))


def _params(*semantics):
    return pltpu.CompilerParams(dimension_semantics=semantics, vmem_limit_bytes=VMEM_LIMIT)


def _inproj_kernel(x_ref, g_ref, w_ref, wt_ref, wup_ref, bg_ref,
                   u_ref, q_ref, kt_ref, lat_ref, v_ref, sg_ref, ss5_ref, sgla_ref):
    x = x_ref[...]
    xg = (x * g_ref[...]).astype(BF16)
    inv = _inv_rms(x)

    def seg(ref, a, b):
        return _dot(xg, ref[:, a:b]) * inv

    a_low = seg(wt_ref, _OFF_ALOW, _W_TAIL_COLS)
    z = _dot(a_low.astype(BF16), wup_ref[...]) + bg_ref[...]
    log_sig = jnp.minimum(z, 0.0) - jnp.log(1.0 + jnp.exp(-jnp.abs(z)))
    lat_ref[...] = (log_sig * (1.0 / GLA_GATE_TEMP)).T
    kt_ref[...] = seg(w_ref, _OFF_K, _OFF_V).T
    u_ref[...] = seg(w_ref, _OFF_U, _OFF_Q).astype(BF16)
    q_ref[...] = (seg(w_ref, _OFF_Q, _OFF_K) * (GLA_DK ** -0.5)).astype(BF16)
    v_ref[...] = seg(w_ref, _OFF_V, _OFF_G).astype(BF16)
    g = seg(w_ref, _OFF_G, _W_MAIN_COLS)
    sg_ref[...] = (g * jax.nn.sigmoid(g)).astype(BF16)
    ss5_ref[...] = jax.nn.sigmoid(seg(wt_ref, _OFF_GS5, _OFF_GGLA)).astype(BF16)
    sgla_ref[...] = jax.nn.sigmoid(seg(wt_ref, _OFF_GGLA, _OFF_ALOW)).astype(BF16)


def _inproj(h, layer, g, w, wt, wup, bg):
    n = h.shape[0]
    row = lambda i: (i, 0)
    col = lambda i: (0, i)
    outs = [
        ((n, S5_WIDTH), BF16, (IN_TILE, S5_WIDTH), row),
        ((n, GLA_KEY), BF16, (IN_TILE, GLA_KEY), row),
        ((GLA_KEY, n), F32, (GLA_KEY, IN_TILE), col),
        ((GLA_KEY, n), F32, (GLA_KEY, IN_TILE), col),
        ((n, GLA_VAL), BF16, (IN_TILE, GLA_VAL), row),
        ((n, GLA_VAL), BF16, (IN_TILE, GLA_VAL), row),
        ((n, D_MODEL), BF16, (IN_TILE, D_MODEL), row),
        ((n, D_MODEL), BF16, (IN_TILE, D_MODEL), row)]
    return pl.pallas_call(
        _inproj_kernel,
        grid=(n // IN_TILE,),
        in_specs=[
            pl.BlockSpec((IN_TILE, D_MODEL), row),
            _layer_spec(g, layer),
            pl.BlockSpec((None, D_MODEL, _W_MAIN_COLS), lambda i: (layer, 0, 0),
                         pipeline_mode=pl.Buffered(1)),
            _layer_spec(wt, layer), _layer_spec(wup, layer), _layer_spec(bg, layer),
        ],
        out_specs=[pl.BlockSpec(blk, imap) for _, _, blk, imap in outs],
        out_shape=[jax.ShapeDtypeStruct(shape, dt) for shape, dt, _, _ in outs],
        compiler_params=_params("arbitrary"),
        name="inproj",
    )(h, g, w, wt, wup, bg)


def _gelu_tanh(x):
    c = math.sqrt(2.0 / math.pi)
    return 0.5 * x * (1.0 + jnp.tanh(c * (x + 0.044715 * (x * x * x))))


def _s5_kernel(u_ref, perm_ref, permt_ref, bq_ref, cq_ref, ar_ref, ai_ref, d_ref, wglu_ref,
               o_ref, buf_ref, st_ref, bmat_ref, cmat_ref):
    @pl.when(pl.program_id(0) == 0)
    def _():
        st_ref[...] = jnp.zeros_like(st_ref)
        bmat_ref[...] = jnp.zeros_like(bmat_ref)
        cmat_ref[...] = jnp.zeros_like(cmat_ref)
        for q in range(S5_PARTS):
            for ri in range(2):
                for gl in range(S5_PART_GROUPS):
                    col = ri * S5_PART_COLS + (gl // 2) * 128
                    bmat_ref[q, gl * S5_GROUP:(gl + 1) * S5_GROUP, col:col + 128] = (
                        bq_ref[q, ri, gl])
                    row = ri * S5_PART_COLS + gl * S5_STATE
                    cmat_ref[q, row:row + S5_STATE, :] = cq_ref[q, ri, gl]

    ut = jnp.concatenate(
        [_dot(perm_ref[...], u_ref[:, s * S5_SUB_T:(s + 1) * S5_SUB_T, :].reshape(
            S5_SUB_ROWS, S5_WIDTH)) for s in range(S5_SUBS)], axis=0)
    ut_bf = ut.astype(BF16)

    part_w = 2 * S5_PART_COLS
    for q in range(S5_PARTS):
        buf_ref[:, q * part_w:(q + 1) * part_w] = _dot(
            ut_bf[:, q * S5_PART_CH:(q + 1) * S5_PART_CH], bmat_ref[q])

    for q in range(S5_PARTS):
        re_cols = slice(q * part_w, q * part_w + S5_PART_COLS)
        im_cols = slice(q * part_w + S5_PART_COLS, (q + 1) * part_w)
        ar = ar_ref[:, q * S5_PART_COLS:(q + 1) * S5_PART_COLS]
        ai = ai_ref[:, q * S5_PART_COLS:(q + 1) * S5_PART_COLS]
        s_re, s_im = st_ref[:, re_cols], st_ref[:, im_cols]
        for t in range(S5_T):
            rows = slice(t * 8, (t + 1) * 8)
            s_re, s_im = (ar * s_re - ai * s_im + buf_ref[rows, re_cols],
                          ar * s_im + ai * s_re + buf_ref[rows, im_cols])
            buf_ref[rows, re_cols] = s_re
            buf_ref[rows, im_cols] = s_im
        st_ref[:, re_cols] = s_re
        st_ref[:, im_cols] = s_im

    ys = [_dot(buf_ref[:, q * part_w:(q + 1) * part_w].astype(BF16), cmat_ref[q])
          for q in range(S5_PARTS)]
    y = jnp.concatenate(ys, axis=-1) + d_ref[...] * ut
    y = _gelu_tanh(y)
    z = _dot(y.astype(BF16), wglu_ref[...])
    out = (z[:, :S5_WIDTH] * jax.nn.sigmoid(z[:, S5_WIDTH:])).astype(BF16)
    for s in range(S5_SUBS):
        o_ref[:, s * S5_SUB_T:(s + 1) * S5_SUB_T, :] = _dot(
            permt_ref[...], out[s * S5_SUB_ROWS:(s + 1) * S5_SUB_ROWS, :]
        ).astype(BF16).reshape(8, S5_SUB_T, S5_WIDTH)


def _s5(u, layer, perm, permt, bq, cq, ar, ai, d, wglu):
    bsz, seq, _ = u.shape
    blk = lambda j: (0, j, 0)
    return pl.pallas_call(
        _s5_kernel,
        grid=(seq // S5_T,),
        in_specs=[
            pl.BlockSpec((bsz, S5_T, S5_WIDTH), blk),
            _const_spec(perm.shape), _const_spec(permt.shape),
            _layer_spec(bq, layer), _layer_spec(cq, layer),
            _layer_spec(ar, layer), _layer_spec(ai, layer),
            _layer_spec(d, layer), _layer_spec(wglu, layer),
        ],
        out_specs=pl.BlockSpec((bsz, S5_T, S5_WIDTH), blk),
        out_shape=jax.ShapeDtypeStruct((bsz, seq, S5_WIDTH), BF16),
        scratch_shapes=[
            pltpu.VMEM((S5_ROWS, S5_COLS), F32),
            pltpu.VMEM((8, S5_COLS), F32),
            pltpu.VMEM((S5_PARTS, S5_PART_CH, 2 * S5_PART_COLS), BF16),
            pltpu.VMEM((S5_PARTS, 2 * S5_PART_COLS, S5_PART_CH), BF16),
        ],
        compiler_params=_params("arbitrary"),
        name="s5",
    )(u, perm, permt, bq, cq, ar, ai, d, wglu)


def _s5_params(a_re, a_im, log_dt, b_re, b_im, c_re, c_im):
    depth = a_re.shape[0]
    dt = jnp.exp(log_dt)[..., None]
    x = a_re * dt
    y = a_im * dt
    ex = jnp.exp(x)
    abar_re = ex * jnp.cos(y)
    abar_im = ex * jnp.sin(y)
    m1_re = jnp.expm1(x) * jnp.cos(y) - 2.0 * jnp.sin(0.5 * y) ** 2
    den = a_re * a_re + a_im * a_im
    coef_re = (m1_re * a_re + abar_im * a_im) / den
    coef_im = (abar_im * a_re - m1_re * a_im) / den
    bb_re = coef_re[..., None] * b_re - coef_im[..., None] * b_im
    bb_im = coef_re[..., None] * b_im + coef_im[..., None] * b_re

    grp = np.arange(S5_GROUPS)
    half_sel = jnp.asarray(grp[:, None] % 2 == np.arange(2)[None, :], F32)
    oct_sel = jnp.asarray(grp[:, None] % 8 == np.arange(8)[None, :], F32)

    def place_b(bb):
        t = jnp.einsum('lgpc,gh->lgchp', bb, half_sel)
        return t.reshape(depth, S5_PARTS, S5_PART_GROUPS, S5_GROUP, 128)

    def place_c(cc):
        t = jnp.einsum('lgcp,go->lgpoc', cc, oct_sel)
        return t.reshape(depth, S5_PARTS, S5_PART_GROUPS, S5_STATE, 128)

    bmat = jnp.stack([place_b(bb_re), place_b(bb_im)], axis=2).astype(BF16)
    cmat = jnp.stack([place_c(c_re), place_c(-c_im)], axis=2).astype(BF16)
    cols = S5_GROUPS * S5_STATE
    ar = jnp.broadcast_to(abar_re.reshape(depth, 1, cols), (depth, 8, cols))
    ai = jnp.broadcast_to(abar_im.reshape(depth, 1, cols), (depth, 8, cols))
    return bmat, cmat, ar, ai


def _gla_kernel(q_ref, kt_ref, lat_ref, v_ref, sg_ref, later_ref, hng_ref, o_ref,
                s_ref, sb_ref):
    @pl.when(pl.program_id(1) == 0)
    def _():
        s_ref[...] = jnp.zeros_like(s_ref)

    later = later_ref[...]

    def emit_output(c):
        rows = slice(c * CHUNK, (c + 1) * CHUNK)
        for h in range(GLA_HEADS):
            vs = slice(h * GLA_DV, (h + 1) * GLA_DV)
            o = _dot(q_ref[rows, h * GLA_DK:(h + 1) * GLA_DK], sb_ref[c, h])
            o = o * _inv_rms(o) * hng_ref[:, vs] * sg_ref[rows, vs].astype(F32)
            o_ref[rows, vs] = o.astype(BF16)

    states = [s_ref[h] for h in range(GLA_HEADS)]
    for grp in range(GLA_T // GLA_CUM_T):
        cols = slice(grp * GLA_CUM_T, (grp + 1) * GLA_CUM_T)
        la = lat_ref[:, cols]
        la_hi, la_lo = _split_bf16(la)
        rest = _dot(la_hi, later) + _dot(la_lo, later)
        k_end_t = (kt_ref[:, cols] * jnp.exp(rest)).astype(BF16)
        for cc in range(GLA_CUM_T // CHUNK):
            c = grp * (GLA_CUM_T // CHUNK) + cc
            first = slice(cc * CHUNK, cc * CHUNK + 1)
            rows = slice(c * CHUNK, (c + 1) * CHUNK)
            decay = jnp.exp(rest[:, first] + la[:, first])
            for h in range(GLA_HEADS):
                ks = slice(h * GLA_DK, (h + 1) * GLA_DK)
                kv = _dot(k_end_t[ks, cc * CHUNK:(cc + 1) * CHUNK],
                          v_ref[rows, h * GLA_DV:(h + 1) * GLA_DV])
                states[h] = states[h] * decay[ks, :] + kv
                sb_ref[c, h] = states[h].astype(BF16)
            if c >= 1:
                emit_output(c - 1)
    emit_output(GLA_NCH - 1)
    for h in range(GLA_HEADS):
        s_ref[h] = states[h]


def _gla(q, kt, lat, v, sg, layer, later, hng, bsz):
    n = q.shape[0]
    nj = n // bsz // GLA_T
    row = lambda b, j: (b * nj + j, 0)
    col = lambda b, j: (0, b * nj + j)
    return pl.pallas_call(
        _gla_kernel,
        grid=(bsz, nj),
        in_specs=[
            pl.BlockSpec((GLA_T, GLA_KEY), row),
            pl.BlockSpec((GLA_KEY, GLA_T), col),
            pl.BlockSpec((GLA_KEY, GLA_T), col),
            pl.BlockSpec((GLA_T, GLA_VAL), row),
            pl.BlockSpec((GLA_T, GLA_VAL), row),
            _const_spec(later.shape), _layer_spec(hng, layer),
        ],
        out_specs=pl.BlockSpec((GLA_T, GLA_VAL), row),
        out_shape=jax.ShapeDtypeStruct((n, GLA_VAL), BF16),
        scratch_shapes=[
            pltpu.VMEM((GLA_HEADS, GLA_DK, GLA_DV), F32),
            pltpu.VMEM((GLA_NCH, GLA_HEADS, GLA_DK, GLA_DV), BF16),
        ],
        compiler_params=_params("arbitrary", "arbitrary"),
        name="gla",
    )(q, kt, lat, v, sg, later, hng)


def _mix_ffn_kernel(h_ref, ys5_ref, ygla_ref, ss5_ref, sgla_ref, wbs5_ref, wbgla_ref, wout_ref,
                    g_ref, wg_ref, wu_ref, wd_ref, fg_ref, o_ref, *, final_norm):
    mixed = (ss5_ref[...].astype(F32) * _dot(ys5_ref[...], wbs5_ref[...])
             + sgla_ref[...].astype(F32) * _dot(ygla_ref[...], wbgla_ref[...]))
    h = h_ref[...] + _dot(mixed.astype(BF16), wout_ref[...])
    hg = (h * g_ref[...]).astype(BF16)
    inv = _inv_rms(h)
    acc = h
    for a, b in FF_CHUNKS:
        gate = _dot(hg, wg_ref[:, a:b]) * inv
        up = _dot(hg, wu_ref[:, a:b]) * inv
        act = (gate * jax.nn.sigmoid(gate) * up).astype(BF16)
        acc = acc + _dot(act, wd_ref[a:b, :])
    if final_norm:
        acc = acc * _inv_rms(acc) * fg_ref[...]
    o_ref[...] = acc


def _mix_ffn(h, ys5, ygla, ss5, sgla, layer, wbs5, wbgla, wout, g, wg, wu, wd, fg, final_norm):
    n = h.shape[0]
    row = lambda i: (i, 0)
    return pl.pallas_call(
        functools.partial(_mix_ffn_kernel, final_norm=final_norm),
        grid=(n // ROW_TILE,),
        in_specs=[
            pl.BlockSpec((ROW_TILE, D_MODEL), row),
            pl.BlockSpec((ROW_TILE, S5_WIDTH), row),
            pl.BlockSpec((ROW_TILE, GLA_VAL), row),
            pl.BlockSpec((ROW_TILE, D_MODEL), row),
            pl.BlockSpec((ROW_TILE, D_MODEL), row),
            _layer_spec(wbs5, layer), _layer_spec(wbgla, layer), _layer_spec(wout, layer),
            _layer_spec(g, layer), _layer_spec(wg, layer), _layer_spec(wu, layer),
            _layer_spec(wd, layer), _const_spec(fg.shape),
        ],
        out_specs=pl.BlockSpec((ROW_TILE, D_MODEL), row),
        out_shape=jax.ShapeDtypeStruct((n, D_MODEL), F32),
        compiler_params=_params("arbitrary"),
        name="mix_ffn_final" if final_norm else "mix_ffn",
    )(h, ys5, ygla, ss5, sgla, wbs5, wbgla, wout, g, wg, wu, wd, fg)


def _pack_w_in(w):
    w = w.astype(BF16)
    a_low = w[..., _W_MAIN_COLS:_W_MAIN_COLS + GLA_GATE_RANK]
    gates = w[..., _W_MAIN_COLS + GLA_GATE_RANK:]
    pad = jnp.zeros(w.shape[:-1] + (_ALOW_PAD - GLA_GATE_RANK,), BF16)
    return w, jnp.concatenate([gates, a_low, pad], axis=-1)


def _perm_matrices():
    p = np.zeros((S5_SUB_ROWS, S5_SUB_ROWS), np.float32)
    t, b = np.meshgrid(np.arange(S5_SUB_T), np.arange(8), indexing='ij')
    p[(t * 8 + b).ravel(), (b * S5_SUB_T + t).ravel()] = 1.0
    return jnp.asarray(p, BF16), jnp.asarray(p.T, BF16)


def kernel(x, attn_norm_g, w_in, s5_a_re, s5_a_im, s5_log_dt, s5_b_re, s5_b_im, s5_c_re, s5_c_im,
           s5_d, s5_w_glu, gla_w_gate_up, gla_b_gate, gla_head_norm_g, w_branch_s5, w_branch_gla,
           w_out, ffn_norm_g, w_ffn_gate, w_ffn_up, w_ffn_down, final_norm_g):
    bsz, seq, _ = x.shape
    n = bsz * seq
    depth = w_in.shape[0]
    perm, permt = _perm_matrices()
    tok = np.arange(GLA_CUM_T)
    later = jnp.asarray((tok[:, None] > tok[None, :])
                        & (tok[:, None] // CHUNK == tok[None, :] // CHUNK), BF16)
    fg = final_norm_g.reshape(1, D_MODEL)

    vec = lambda t: t.reshape(depth, 1, -1)
    w_main, w_tail = _pack_w_in(w_in)
    wup = jnp.pad(gla_w_gate_up,
                  ((0, 0), (0, _ALOW_PAD - GLA_GATE_RANK), (0, 0))).astype(BF16)
    bmat, cmat, ar, ai = _s5_params(s5_a_re, s5_a_im, s5_log_dt, s5_b_re, s5_b_im,
                                    s5_c_re, s5_c_im)
    wglu = s5_w_glu.astype(BF16)
    wbs5, wbgla, wout = (w_branch_s5.astype(BF16), w_branch_gla.astype(BF16),
                         w_out.astype(BF16))
    wg, wu, wd = w_ffn_gate.astype(BF16), w_ffn_up.astype(BF16), w_ffn_down.astype(BF16)
    attn_g, ffn_g, bgate, hng, s5d = (vec(attn_norm_g), vec(ffn_norm_g), vec(gla_b_gate),
                                      vec(gla_head_norm_g), vec(s5_d))

    h = x.reshape(n, D_MODEL)
    for l in range(depth):
        u, q, kt, lat, v, sg, ss5, sgla = _inproj(h, l, attn_g, w_main, w_tail, wup, bgate)
        ys5 = _s5(u.reshape(bsz, seq, S5_WIDTH), l, perm, permt, bmat, cmat, ar, ai,
                  s5d, wglu).reshape(n, S5_WIDTH)
        ygla = _gla(q, kt, lat, v, sg, l, later, hng, bsz)
        h = _mix_ffn(h, ys5, ygla, ss5, sgla, l, wbs5, wbgla, wout, ffn_g, wg, wu, wd, fg,
                     final_norm=(l == depth - 1))
    return h.reshape(bsz, seq, D_MODEL)
```
